```python
import jax, jax.numpy as jnp
from jax import lax
import numpy as np

D_MODEL = 4096
BATCH = 4
SEQ = 2048
DEPTH = 1

N_META = 16
CHUNK = 64
RET_HEADS = 8
RET_DK = D_MODEL // 16
RET_DV = D_MODEL // RET_HEADS
GLA_HEADS = 4
GLA_DK = D_MODEL // 8
GLA_DV = D_MODEL // GLA_HEADS
GLA_RANK = 16
GLA_GATE_TAU = 16.0
D_FF = 11008
CONV_W = 3
ROPE_BASE = 10000.0
EPS = 1e-6

RET_QK = RET_HEADS * RET_DK
RET_V = RET_HEADS * RET_DV
GLA_QK = GLA_HEADS * GLA_DK
GLA_V = GLA_HEADS * GLA_DV
IN_WIDTHS = (RET_QK, RET_QK, RET_V, RET_V,
             GLA_QK, GLA_QK, GLA_V, GLA_V,
             GLA_RANK,
             D_MODEL, D_MODEL)
W_IN_COLS = 4 * RET_QK // 2 + 2 * RET_V + 2 * GLA_QK + 2 * GLA_V + GLA_RANK + 2 * D_MODEL

kernel_name = "hybrid_retnet_gla_convffn_meta"


def _rmsnorm(x, g):
    xf = x.astype(jnp.float32)
    y = xf * lax.rsqrt(jnp.mean(xf * xf, axis=-1, keepdims=True) + EPS)
    return (y * g.astype(jnp.float32)).astype(x.dtype)


def _rope(x, pos):
    half = x.shape[-1] // 2
    inv_freq = ROPE_BASE ** (-jnp.arange(half, dtype=jnp.float32) / half)
    ang = pos.astype(jnp.float32)[..., None] * inv_freq
    cos = jnp.cos(ang)[:, :, None, :]
    sin = jnp.sin(ang)[:, :, None, :]
    x1 = x[..., :half].astype(jnp.float32)
    x2 = x[..., half:].astype(jnp.float32)
    return jnp.concatenate([x1 * cos - x2 * sin, x2 * cos + x1 * sin], axis=-1).astype(x.dtype)


def _to_chunks(t):
    B, L, H, d = t.shape
    t = jnp.pad(t, ((0, 0), (CHUNK - N_META, 0), (0, 0), (0, 0)))
    n = t.shape[1] // CHUNK
    return t.reshape(B, n, CHUNK, H, d).transpose(1, 0, 3, 2, 4)


def _from_chunks(o):
    n, B, H, C, d = o.shape
    return o.transpose(1, 0, 3, 2, 4).reshape(B, n * C, H, d)[:, CHUNK - N_META:]


def _retention_chunked(q, k, v):
    H, dk, dv = q.shape[2], q.shape[3], v.shape[3]
    B = q.shape[0]
    qc, kc, vc = _to_chunks(q), _to_chunks(k), _to_chunks(v)
    log_gamma = jnp.log1p(-jnp.exp2(-5.0 - jnp.arange(H, dtype=jnp.float32)))
    idx = jnp.arange(CHUNK)
    rel = idx[:, None] - idx[None, :]
    causal = rel >= 0
    decay_intra = jnp.where(causal, jnp.exp(log_gamma[:, None, None] * jnp.where(causal, rel, 0)), 0.0)
    decay_q = jnp.exp(log_gamma[:, None] * (idx + 1))[None, :, :, None]
    decay_k = jnp.exp(log_gamma[:, None] * (CHUNK - 1 - idx))[None, :, :, None]
    decay_chunk = jnp.exp(log_gamma * CHUNK)[None, :, None, None]

    def step(state, inp):
        qi, ki, vi = inp
        scores = jnp.einsum('bhqd,bhkd->bhqk', qi, ki) * decay_intra
        inner = jnp.einsum('bhqk,bhkv->bhqv', scores, vi)
        cross = jnp.einsum('bhqd,bhdv->bhqv', qi, state) * decay_q
        new_state = state * decay_chunk + jnp.einsum('bhkd,bhkv->bhdv', ki * decay_k, vi)
        return new_state, (inner + cross).astype(jnp.float32)

    state0 = jnp.zeros((B, H, dk, dv), jnp.float32)
    _, out = lax.scan(step, state0, (qc, kc, vc))
    return _from_chunks(out)


def _gla_chunked(q, k, v, log_a):
    B, H, dk, dv = q.shape[0], q.shape[2], q.shape[3], v.shape[3]
    qc, kc, vc, gc = _to_chunks(q), _to_chunks(k), _to_chunks(v), _to_chunks(log_a)
    idx = jnp.arange(CHUNK)
    causal = (idx[:, None] >= idx[None, :])[None, None, :, :, None]

    def step(state, inp):
        qi, ki, vi, gi = inp
        b = jnp.cumsum(gi.astype(jnp.float32), axis=2)
        cross = jnp.einsum('bhqd,bhdv->bhqv', qi * jnp.exp(b), state)
        diff = b[:, :, :, None, :] - b[:, :, None, :, :]
        w = jnp.where(causal, jnp.exp(jnp.where(causal, diff, 0.0)), 0.0)
        scores = jnp.einsum('bhqd,bhkd,bhqkd->bhqk', qi, ki, w)
        inner = jnp.einsum('bhqk,bhkv->bhqv', scores, vi)
        b_last = b[:, :, -1:, :]
        new_state = state * jnp.exp(b_last)[:, :, 0, :, None] + jnp.einsum(
            'bhkd,bhkv->bhdv', ki * jnp.exp(b_last - b), vi)
        return new_state, (inner + cross).astype(jnp.float32)

    state0 = jnp.zeros((B, H, dk, dv), jnp.float32)
    _, out = lax.scan(step, state0, (qc, kc, vc, gc))
    return _from_chunks(out)


def _hybrid_mixer(h, pos, w_in, w_gate_up, b_gate, g_ret, g_gla, w_out):
    B, L, _ = h.shape
    proj = h @ w_in
    splits = np.cumsum(IN_WIDTHS)[:-1].tolist()
    q_r, k_r, v_r, o_r, q_g, k_g, v_g, o_g, z_g, m_r, m_g = jnp.split(proj, splits, axis=-1)

    q = _rope(q_r.reshape(B, L, RET_HEADS, RET_DK), pos)
    k = _rope(k_r.reshape(B, L, RET_HEADS, RET_DK), pos) * (RET_DK ** -0.5)
    v = v_r.reshape(B, L, RET_HEADS, RET_DV)
    y = _retention_chunked(q, k, v)
    mu = jnp.mean(y, axis=-1, keepdims=True)
    var = jnp.mean(jnp.square(y - mu), axis=-1, keepdims=True)
    y = ((y - mu) * lax.rsqrt(var + EPS)).reshape(B, L, RET_V) * g_ret.astype(jnp.float32)
    y_ret = y.astype(h.dtype) * jax.nn.silu(o_r)

    log_a = jax.nn.log_sigmoid((z_g @ w_gate_up + b_gate).astype(jnp.float32)) / GLA_GATE_TAU
    q = q_g.reshape(B, L, GLA_HEADS, GLA_DK) * (GLA_DK ** -0.5)
    k = k_g.reshape(B, L, GLA_HEADS, GLA_DK)
    v = v_g.reshape(B, L, GLA_HEADS, GLA_DV)
    y = _gla_chunked(q, k, v, log_a.reshape(B, L, GLA_HEADS, GLA_DK))
    y = (y * lax.rsqrt(jnp.mean(y * y, axis=-1, keepdims=True) + EPS)).reshape(B, L, GLA_V)
    y_gla = (y * g_gla.astype(jnp.float32)).astype(h.dtype) * jax.nn.silu(o_g)

    merged = jax.nn.sigmoid(m_r) * y_ret + jax.nn.sigmoid(m_g) * y_gla
    return (merged @ w_out).astype(h.dtype)


def _conv_ffn(h, w_ffn_in, conv_w, conv_b, w_ffn_out):
    L = h.shape[1]
    up, gate = jnp.split(h @ w_ffn_in, 2, axis=-1)
    up_p = jnp.pad(up, ((0, 0), (CONV_W - 1, 0), (0, 0)))
    c = conv_b
    for j in range(CONV_W):
        c = c + conv_w[j] * up_p[:, j:j + L]
    return ((jax.nn.silu(c) * gate) @ w_ffn_out).astype(h.dtype)


def setup_inputs(seed: int = 0) -> dict:
    key = jax.random.key(seed)
    ks = jax.random.split(key, 16)
    f32 = jnp.float32
    nrm = lambda k, s, sc: jax.random.normal(k, s, f32) * sc
    return {
        "x": nrm(ks[0], (BATCH, SEQ, D_MODEL), 1.0),
        "positions": jnp.broadcast_to(jnp.arange(SEQ, dtype=jnp.int32), (BATCH, SEQ)),
        "meta_tokens": nrm(ks[1], (N_META, D_MODEL), 1.0),
        "attn_norm": 1.0 + nrm(ks[2], (DEPTH, D_MODEL), 0.02),
        "w_in": nrm(ks[3], (DEPTH, D_MODEL, W_IN_COLS), D_MODEL ** -0.5),
        "w_gate_up": nrm(ks[4], (DEPTH, GLA_RANK, GLA_QK), GLA_RANK ** -0.5),
        "b_gate": nrm(ks[5], (DEPTH, GLA_QK), 0.01),
        "ret_norm": 1.0 + nrm(ks[6], (DEPTH, RET_V), 0.02),
        "gla_norm": 1.0 + nrm(ks[7], (DEPTH, GLA_V), 0.02),
        "w_out": nrm(ks[8], (DEPTH, D_MODEL, D_MODEL), D_MODEL ** -0.5),
        "ffn_norm": 1.0 + nrm(ks[9], (DEPTH, D_MODEL), 0.02),
        "w_ffn_in": nrm(ks[10], (DEPTH, D_MODEL, 2 * D_FF), D_MODEL ** -0.5),
        "conv_w": nrm(ks[11], (DEPTH, CONV_W, D_FF), CONV_W ** -0.5),
        "conv_b": nrm(ks[12], (DEPTH, D_FF), 0.01),
        "w_ffn_out": nrm(ks[13], (DEPTH, D_FF, D_MODEL), D_FF ** -0.5),
        "final_norm": 1.0 + nrm(ks[14], (D_MODEL,), 0.02),
    }


def reference(x, positions, meta_tokens, attn_norm, w_in, w_gate_up, b_gate, ret_norm, gla_norm,
              w_out, ffn_norm, w_ffn_in, conv_w, conv_b, w_ffn_out, final_norm):
    B = x.shape[0]
    meta = jnp.broadcast_to(meta_tokens[None].astype(x.dtype), (B, N_META, D_MODEL))
    h = jnp.concatenate([meta, x], axis=1)
    pos = jnp.concatenate([jnp.broadcast_to(jnp.arange(N_META, dtype=jnp.int32), (B, N_META)),
                           positions.astype(jnp.int32) + N_META], axis=1)
    for i in range(DEPTH):
        h = h + _hybrid_mixer(_rmsnorm(h, attn_norm[i]), pos, w_in[i], w_gate_up[i], b_gate[i],
                              ret_norm[i], gla_norm[i], w_out[i])
        h = h + _conv_ffn(_rmsnorm(h, ffn_norm[i]), w_ffn_in[i], conv_w[i], conv_b[i], w_ffn_out[i])
    h = _rmsnorm(h, final_norm)
    return h[:, N_META:]
```

```python
import functools

import jax
import jax.numpy as jnp
import numpy as np
from jax import lax
from jax.experimental import pallas as pl
from jax.experimental.pallas import tpu as pltpu

F32 = jnp.float32
BF16 = jnp.bfloat16

D_MODEL = 4096
N_META = 16
PAD_ROWS = 48
RET_HEADS, RET_DK, RET_DV = 8, 256, 512
GLA_HEADS, GLA_DK, GLA_DV = 4, 512, 1024
GLA_RANK = 16
GLA_GATE_TAU = 16.0
D_FF = 11008
CONV_W = 3
ROPE_BASE = 10000.0
EPS = 1e-6
RET_QK = RET_HEADS * RET_DK
N_MAIN = 24576

VMEM_LIMIT_V7X = 56 * 1024 * 1024
LANES = 128
META_CHUNK = 64
RET_CHUNK = 256
GLA_CHUNK = 64
GLA_SUB = 16
GLA_SEQ_BLOCK = 704


def _params(sem):
    return pltpu.CompilerParams(dimension_semantics=sem, vmem_limit_bytes=VMEM_LIMIT_V7X)


def _sigmoid(x):
    return 1.0 / (1.0 + jnp.exp(-x))


def _iota(shape, dim):
    return lax.broadcasted_iota(jnp.int32, shape, dim)


def _rmsnorm_kernel(x_ref, g_ref, o_ref):
    x = x_ref[...]
    y = x * lax.rsqrt(jnp.mean(x * x, axis=-1, keepdims=True) + EPS)
    o_ref[...] = (y * g_ref[...]).astype(o_ref.dtype)


def _rmsnorm(x, g, *, rows, out_dtype):
    m, d = x.shape
    return pl.pallas_call(
        _rmsnorm_kernel,
        out_shape=jax.ShapeDtypeStruct((m, d), out_dtype),
        grid=(m // rows,),
        in_specs=[pl.BlockSpec((rows, d), lambda i: (i, 0)),
                  pl.BlockSpec((1, d), lambda i: (0, 0))],
        out_specs=pl.BlockSpec((rows, d), lambda i: (i, 0)),
        compiler_params=_params(("parallel",)),
        name="rmsnorm",
    )(x, g.reshape(1, d))


def _final_norm_kernel(x_ref, g_ref, o_ref):
    x = x_ref[0]
    y = x * lax.rsqrt(jnp.mean(x * x, axis=-1, keepdims=True) + EPS)
    o_ref[0] = y * g_ref[...]


def _final_norm(h, g, *, batch, rows_per_batch, seq):
    d = h.shape[1]
    blk = META_CHUNK
    return pl.pallas_call(
        _final_norm_kernel,
        out_shape=jax.ShapeDtypeStruct((batch, seq, d), F32),
        grid=(batch, seq // blk),
        in_specs=[pl.BlockSpec((1, blk, d), lambda b, j: (b, j + 1, 0)),
                  pl.BlockSpec((1, d), lambda b, j: (0, 0))],
        out_specs=pl.BlockSpec((1, blk, d), lambda b, j: (b, j, 0)),
        compiler_params=_params(("parallel", "parallel")),
        name="final_norm",
    )(h.reshape(batch, rows_per_batch, d), g.reshape(1, d))


def _mm_kernel(a_ref, w_ref, o_ref):
    o_ref[...] = jnp.dot(a_ref[...], w_ref[...], preferred_element_type=F32).astype(o_ref.dtype)


def _mm_res_kernel(a_ref, w_ref, r_ref, o_ref):
    acc = jnp.dot(a_ref[...], w_ref[...], preferred_element_type=F32)
    o_ref[...] = (r_ref[...] + acc).astype(o_ref.dtype)


def _matmul(a, w, *, tm, tn, out_dtype, residual=None, name):
    m, k = a.shape
    n = w.shape[1]
    in_specs = [pl.BlockSpec((tm, k), lambda i, j: (i, 0)),
                pl.BlockSpec((k, tn), lambda i, j: (0, j))]
    args = [a, w]
    kern = _mm_kernel
    if residual is not None:
        in_specs.append(pl.BlockSpec((tm, tn), lambda i, j: (i, j)))
        args.append(residual)
        kern = _mm_res_kernel
    return pl.pallas_call(
        kern,
        out_shape=jax.ShapeDtypeStruct((m, n), out_dtype),
        grid=(m // tm, n // tn),
        in_specs=in_specs,
        out_specs=pl.BlockSpec((tm, tn), lambda i, j: (i, j)),
        compiler_params=_params(("parallel", "parallel")),
        name=name,
    )(*args)


def _rope_table_kernel(pos_ref, invf_ref, cos_ref, sin_ref):
    ang = pos_ref[...].astype(F32) * invf_ref[...]
    cos_ref[...] = jnp.cos(ang)
    sin_ref[...] = jnp.sin(ang)


def _rope_tables(pos_col, inv_freq, *, rows):
    m = pos_col.shape[0]
    half = inv_freq.shape[1]
    out = jax.ShapeDtypeStruct((m, half), F32)
    return pl.pallas_call(
        _rope_table_kernel,
        out_shape=(out, out),
        grid=(m // rows,),
        in_specs=[pl.BlockSpec((rows, 1), lambda i: (i, 0)),
                  pl.BlockSpec((1, half), lambda i: (0, 0))],
        out_specs=(pl.BlockSpec((rows, half), lambda i: (i, 0)),
                   pl.BlockSpec((rows, half), lambda i: (i, 0))),
        compiler_params=_params(("parallel",)),
        name="rope_tables",
    )(pos_col, inv_freq)


def _ret_chunk(r0, n, lg, q_ref, k_ref, v_ref, o_ref, m_ref, cos_ref, sin_ref, g_ref, out_ref, s_ref):
    rows = pl.ds(r0, n)
    half = RET_DK // 2
    cos = cos_ref[rows, :]
    sin = sin_ref[rows, :]

    def rope(ref):
        x = ref[rows, :].astype(F32)
        x1, x2 = x[:, :half], x[:, half:]
        return jnp.concatenate([x1 * cos - x2 * sin, x2 * cos + x1 * sin], axis=-1)

    q = rope(q_ref)
    k = rope(k_ref) * (RET_DK ** -0.5)
    v = v_ref[rows, :]

    rel = (_iota((n, n), 0) - _iota((n, n), 1)).astype(F32)
    d_intra = jnp.where(rel >= 0, jnp.exp(lg * jnp.maximum(rel, 0.0)), 0.0)
    ridx = _iota((n, 1), 0).astype(F32)
    d_q = jnp.exp(lg * (ridx + 1.0))
    d_k = jnp.exp(lg * (n - 1.0 - ridx))
    d_chunk = jnp.exp(lg * float(n))

    qb = q.astype(BF16)
    scores = lax.dot_general(qb, k.astype(BF16), (((1,), (1,)), ((), ())),
                             preferred_element_type=F32) * d_intra
    inner = jnp.dot(scores.astype(BF16), v, preferred_element_type=F32)
    state = s_ref[...]
    cross = jnp.dot(qb, state.astype(BF16), preferred_element_type=F32) * d_q
    y = inner + cross
    upd = lax.dot_general((k * d_k).astype(BF16), v, (((0,), (0,)), ((), ())),
                          preferred_element_type=F32)
    s_ref[...] = state * d_chunk + upd

    mu = jnp.mean(y, axis=-1, keepdims=True)
    yc = y - mu
    var = jnp.mean(yc * yc, axis=-1, keepdims=True)
    yn = yc * lax.rsqrt(var + EPS) * g_ref[...]
    o = o_ref[rows, :].astype(F32)
    m = m_ref[rows, :].astype(F32)
    out_ref[rows, :] = (_sigmoid(m) * (yn * (o * _sigmoid(o)))).astype(out_ref.dtype)


def _ret_kernel(lg_ref, q_ref, k_ref, v_ref, o_ref, m_ref, cos_ref, sin_ref, g_ref, out_ref, s_ref):
    lg = lg_ref[0][:, :1]
    refs = (q_ref, k_ref, v_ref, o_ref, m_ref, cos_ref, sin_ref, g_ref, out_ref, s_ref)
    s_ref[...] = jnp.zeros_like(s_ref)
    _ret_chunk(0, META_CHUNK, lg, *refs)
    n_chunks = (q_ref.shape[0] - META_CHUNK) // RET_CHUNK

    def body(c, carry):
        r0 = pl.multiple_of(META_CHUNK + c * RET_CHUNK, META_CHUNK)
        _ret_chunk(r0, RET_CHUNK, lg, *refs)
        return carry

    lax.fori_loop(0, n_chunks, body, 0)


def _retention(proj, gates, cos, sin, g_ret, *, batch, rows):
    m = proj.shape[0]
    log_gamma = jnp.log1p(-jnp.exp2(-5.0 - jnp.arange(RET_HEADS, dtype=F32)))
    lg = jnp.broadcast_to(log_gamma[:, None, None], (RET_HEADS, 1, LANES))
    kq, kv = RET_QK // RET_DK, (2 * RET_QK) // RET_DV
    return pl.pallas_call(
        _ret_kernel,
        out_shape=jax.ShapeDtypeStruct((m, D_MODEL), BF16),
        grid=(batch, RET_HEADS),
        in_specs=[
            pl.BlockSpec((1, 1, LANES), lambda b, h: (h, 0, 0)),
            pl.BlockSpec((rows, RET_DK), lambda b, h: (b, h)),
            pl.BlockSpec((rows, RET_DK), lambda b, h: (b, kq + h)),
            pl.BlockSpec((rows, RET_DV), lambda b, h: (b, kv + h)),
            pl.BlockSpec((rows, RET_DV), lambda b, h: (b, kv + RET_HEADS + h)),
            pl.BlockSpec((rows, RET_DV), lambda b, h: (b, h)),
            pl.BlockSpec((rows, RET_DK // 2), lambda b, h: (b, 0)),
            pl.BlockSpec((rows, RET_DK // 2), lambda b, h: (b, 0)),
            pl.BlockSpec((1, RET_DV), lambda b, h: (0, h)),
        ],
        out_specs=pl.BlockSpec((rows, RET_DV), lambda b, h: (b, h)),
        scratch_shapes=[pltpu.VMEM((RET_DK, RET_DV), F32)],
        compiler_params=_params(("parallel", "parallel")),
        name="retention",
    )(lg, proj, proj, proj, proj, gates, cos, sin, g_ret.reshape(1, D_MODEL))


def _gla_chunk(r0, q_ref, k_ref, v_ref, o_ref, m_ref, z_ref, wgu_ref, bg_ref, g_ref, yr_ref,
               out_ref, s_ref):
    c_len, sub = GLA_CHUNK, GLA_SUB
    rows = pl.ds(r0, c_len)

    u = jnp.dot(z_ref[rows, :].astype(BF16), wgu_ref[...], preferred_element_type=F32) + bg_ref[...]
    log_a = (jnp.minimum(u, 0.0) - jnp.log1p(jnp.exp(-jnp.abs(u)))) * (1.0 / GLA_GATE_TAU)
    tri = (_iota((c_len, c_len), 0) >= _iota((c_len, c_len), 1)).astype(BF16)
    hi = log_a.astype(BF16)
    r1 = log_a - hi.astype(F32)
    mid = r1.astype(BF16)
    lo = (r1 - mid.astype(F32)).astype(BF16)
    b = (jnp.dot(tri, hi, preferred_element_type=F32)
         + jnp.dot(tri, mid, preferred_element_type=F32)
         + jnp.dot(tri, lo, preferred_element_type=F32))

    q = q_ref[rows, :].astype(F32) * (GLA_DK ** -0.5)
    k = k_ref[rows, :].astype(F32)
    v = v_ref[rows, :]

    state = s_ref[...]
    cross = jnp.dot((q * jnp.exp(b)).astype(BF16), state.astype(BF16), preferred_element_type=F32)

    key_row = _iota((c_len, 1), 0)
    lane = _iota((sub, c_len), 1)
    qrow = _iota((sub, c_len), 0)
    blocks = []
    for s in range(c_len // sub):
        sl = slice(sub * s, sub * (s + 1))
        bs, qs, ks = b[sl], q[sl], k[sl]
        acc = jnp.zeros((sub, c_len), F32)
        for j in range(sub):
            w = jnp.exp(jnp.minimum(bs - bs[j:j + 1, :], 0.0))
            col = jnp.sum(qs * w * ks[j:j + 1, :], axis=-1, keepdims=True)
            acc = jnp.where((lane == sub * s + j) & (qrow >= j), col, acc)
        if s > 0:
            b_ref_row = b[sub * s - 1:sub * s, :]
            qt = qs * jnp.exp(bs - b_ref_row)
            earlier = key_row < sub * s
            kt = jnp.where(earlier, k * jnp.exp(jnp.where(earlier, b_ref_row - b, 0.0)), 0.0)
            acc = acc + lax.dot_general(qt.astype(BF16), kt.astype(BF16), (((1,), (1,)), ((), ())),
                                        preferred_element_type=F32)
        blocks.append(acc)
    scores = jnp.concatenate(blocks, axis=0)
    y = jnp.dot(scores.astype(BF16), v, preferred_element_type=F32) + cross

    b_last = b[c_len - 1:c_len, :]
    kd = k * jnp.exp(b_last - b)
    upd = lax.dot_general(kd.astype(BF16), v, (((0,), (0,)), ((), ())), preferred_element_type=F32)
    decay_col = jnp.broadcast_to(jnp.exp(b_last), (LANES, GLA_DK)).T
    s_ref[...] = state * jnp.tile(decay_col, (1, GLA_DV // LANES)) + upd

    yn = y * lax.rsqrt(jnp.mean(y * y, axis=-1, keepdims=True) + EPS) * g_ref[...]
    o = o_ref[rows, :].astype(F32)
    m = m_ref[rows, :].astype(F32)
    merged = _sigmoid(m) * (yn * (o * _sigmoid(o))) + yr_ref[rows, :].astype(F32)
    out_ref[rows, :] = merged.astype(out_ref.dtype)


def _gla_kernel(q_ref, k_ref, v_ref, o_ref, m_ref, z_ref, wgu_ref, bg_ref, g_ref, yr_ref,
                out_ref, s_ref):
    @pl.when(pl.program_id(2) == 0)
    def _():
        s_ref[...] = jnp.zeros_like(s_ref)

    refs = (q_ref, k_ref, v_ref, o_ref, m_ref, z_ref, wgu_ref, bg_ref, g_ref, yr_ref, out_ref, s_ref)

    def body(c, carry):
        _gla_chunk(pl.multiple_of(c * GLA_CHUNK, GLA_CHUNK), *refs)
        return carry

    lax.fori_loop(0, q_ref.shape[0] // GLA_CHUNK, body, 0)


def _gla_merge(proj, gates, z, wgu, bg, g_gla, y_ret, *, batch, rows):
    m = proj.shape[0]
    bs = GLA_SEQ_BLOCK
    nsb = rows // bs
    cq = (2 * RET_QK + 2 * D_MODEL) // GLA_DK
    ck = cq + GLA_HEADS
    cv = (2 * RET_QK + 2 * D_MODEL + 2 * GLA_HEADS * GLA_DK) // GLA_DV
    co = cv + GLA_HEADS

    def row(b, j):
        return b * nsb + j

    return pl.pallas_call(
        _gla_kernel,
        out_shape=jax.ShapeDtypeStruct((m, D_MODEL), BF16),
        grid=(batch, GLA_HEADS, nsb),
        in_specs=[
            pl.BlockSpec((bs, GLA_DK), lambda b, h, j: (row(b, j), cq + h)),
            pl.BlockSpec((bs, GLA_DK), lambda b, h, j: (row(b, j), ck + h)),
            pl.BlockSpec((bs, GLA_DV), lambda b, h, j: (row(b, j), cv + h)),
            pl.BlockSpec((bs, GLA_DV), lambda b, h, j: (row(b, j), co + h)),
            pl.BlockSpec((bs, GLA_DV), lambda b, h, j: (row(b, j), GLA_HEADS + h)),
            pl.BlockSpec((bs, LANES), lambda b, h, j: (row(b, j), 0)),
            pl.BlockSpec((LANES, GLA_DK), lambda b, h, j: (0, h)),
            pl.BlockSpec((1, GLA_DK), lambda b, h, j: (0, h)),
            pl.BlockSpec((1, GLA_DV), lambda b, h, j: (0, h)),
            pl.BlockSpec((bs, GLA_DV), lambda b, h, j: (row(b, j), h)),
        ],
        out_specs=pl.BlockSpec((bs, GLA_DV), lambda b, h, j: (row(b, j), h)),
        scratch_shapes=[pltpu.VMEM((GLA_DK, GLA_DV), F32)],
        compiler_params=_params(("parallel", "parallel", "arbitrary")),
        name="gla_merge",
    )(proj, proj, proj, proj, gates, z, wgu, bg, g_gla.reshape(1, D_MODEL), y_ret)


def _ffn_in_kernel(a_ref, wu_ref, wg_ref, cw_ref, cb_ref, hid_ref, buf_ref, carry_ref):
    i = pl.program_id(0)
    j = pl.program_id(1)
    a = a_ref[...]
    up = jnp.dot(a, wu_ref[...], preferred_element_type=F32)
    gate = jnp.dot(a, wg_ref[...], preferred_element_type=F32)
    tm = up.shape[0]

    @pl.when(i == 0)
    def _():
        buf_ref[0:8, :] = jnp.zeros((8, up.shape[1]), F32)

    @pl.when(i > 0)
    def _():
        buf_ref[0:8, :] = carry_ref[j]

    buf_ref[8:8 + tm, :] = up
    carry_ref[j] = buf_ref[tm:tm + 8, :]

    cw = cw_ref[...]
    c = (cb_ref[...] + cw[0:1, :] * buf_ref[6:6 + tm, :] + cw[1:2, :] * buf_ref[7:7 + tm, :]
         + cw[2:3, :] * up)
    hid_ref[...] = (c * _sigmoid(c) * gate).astype(hid_ref.dtype)


def _ffn_in(a, w_ffn_in, conv_w, conv_b, *, tm, tf):
    m, k = a.shape
    nf = D_FF // tf
    return pl.pallas_call(
        _ffn_in_kernel,
        out_shape=jax.ShapeDtypeStruct((m, D_FF), BF16),
        grid=(m // tm, nf),
        in_specs=[
            pl.BlockSpec((tm, k), lambda i, j: (i, 0)),
            pl.BlockSpec((k, tf), lambda i, j: (0, j)),
            pl.BlockSpec((k, tf), lambda i, j: (0, nf + j)),
            pl.BlockSpec((CONV_W, tf), lambda i, j: (0, j)),
            pl.BlockSpec((1, tf), lambda i, j: (0, j)),
        ],
        out_specs=pl.BlockSpec((tm, tf), lambda i, j: (i, j)),
        scratch_shapes=[pltpu.VMEM((tm + 8, tf), F32), pltpu.VMEM((nf, 8, tf), F32)],
        compiler_params=_params(("arbitrary", "arbitrary")),
        name="ffn_in_conv",
    )(a, w_ffn_in, w_ffn_in, conv_w, conv_b.reshape(1, D_FF))


def kernel(x, positions, meta_tokens, attn_norm, w_in, w_gate_up, b_gate, ret_norm, gla_norm,
           w_out, ffn_norm, w_ffn_in, conv_w, conv_b, w_ffn_out, final_norm):
    batch, seq, d = x.shape
    rows = PAD_ROWS + N_META + seq
    m = batch * rows

    meta = jnp.broadcast_to(meta_tokens[None].astype(x.dtype), (batch, N_META, d))
    h0 = jnp.concatenate([jnp.zeros((batch, PAD_ROWS, d), x.dtype), meta, x], axis=1).reshape(m, d)
    pos = jnp.concatenate([
        jnp.zeros((batch, PAD_ROWS), jnp.int32),
        jnp.broadcast_to(jnp.arange(N_META, dtype=jnp.int32), (batch, N_META)),
        positions.astype(jnp.int32) + N_META], axis=1).reshape(m, 1)
    half = RET_DK // 2
    inv_freq = (ROPE_BASE ** (-jnp.arange(half, dtype=F32) / half)).reshape(1, half)

    w_in0 = w_in[0]
    w_main = w_in0[:, :N_MAIN].astype(BF16)
    w_z = jnp.pad(w_in0[:, N_MAIN:N_MAIN + GLA_RANK], ((0, 0), (0, LANES - GLA_RANK))).astype(BF16)
    w_gates = w_in0[:, N_MAIN + GLA_RANK:].astype(BF16)
    wgu = jnp.pad(w_gate_up[0], ((0, LANES - GLA_RANK), (0, 0))).astype(BF16)
    w_out_b = w_out[0].astype(BF16)
    w_ffn_in_b = w_ffn_in[0].astype(BF16)
    w_ffn_out_b = w_ffn_out[0].astype(BF16)

    cos, sin = _rope_tables(pos, inv_freq, rows=rows)

    hn = _rmsnorm(h0, attn_norm[0], rows=528, out_dtype=BF16)
    proj = _matmul(hn, w_main, tm=1056, tn=1024, out_dtype=BF16, name="in_proj")
    gates = _matmul(hn, w_gates, tm=1056, tn=1024, out_dtype=BF16, name="in_proj_gates")
    z = _matmul(hn, w_z, tm=1056, tn=LANES, out_dtype=F32, name="in_proj_z")

    y_ret = _retention(proj, gates, cos, sin, ret_norm[0], batch=batch, rows=rows)
    merged = _gla_merge(proj, gates, z, wgu, b_gate[0].reshape(1, -1), gla_norm[0], y_ret,
                        batch=batch, rows=rows)

    h1 = _matmul(merged, w_out_b, tm=1056, tn=512, out_dtype=F32, residual=h0, name="out_proj")
    h1n = _rmsnorm(h1, ffn_norm[0], rows=528, out_dtype=BF16)
    hidden = _ffn_in(h1n, w_ffn_in_b, conv_w[0], conv_b[0], tm=1056, tf=256)
    h2 = _matmul(hidden, w_ffn_out_b, tm=768, tn=256, out_dtype=F32, residual=h1, name="ffn_out")
    return _final_norm(h2, final_norm, batch=batch, rows_per_batch=rows, seq=seq)
```

```python
import functools

import jax
import jax.numpy as jnp
import numpy as np
from jax import lax
from jax.experimental import pallas as pl
from jax.experimental.pallas import tpu as pltpu

F32 = jnp.float32
BF16 = jnp.bfloat16

D_MODEL = 4096
N_META = 16
PAD_ROWS = 48
RET_HEADS, RET_DK, RET_DV = 8, 256, 512
GLA_HEADS, GLA_DK, GLA_DV = 4, 512, 1024
GLA_RANK = 16
GLA_GATE_TAU = 16.0
D_FF = 11008
CONV_W = 3
ROPE_BASE = 10000.0
EPS = 1e-6
RET_QK = RET_HEADS * RET_DK
N_MAIN = 24576

VMEM_LIMIT_V7X = 56 * 1024 * 1024
LANES = 128
META_CHUNK = 64
RET_CHUNK = 256
GLA_CHUNK = 64
GLA_SUB = 16
GLA_SEQ_BLOCK = 704


def _params(sem):
    return pltpu.CompilerParams(dimension_semantics=sem, vmem_limit_bytes=VMEM_LIMIT_V7X)


def _sigmoid(x):
    return 1.0 / (1.0 + jnp.exp(-x))


def _iota(shape, dim):
    return lax.broadcasted_iota(jnp.int32, shape, dim)


def _rmsnorm_kernel(x_ref, g_ref, o_ref):
    x = x_ref[...]
    y = x * lax.rsqrt(jnp.mean(x * x, axis=-1, keepdims=True) + EPS)
    o_ref[...] = (y * g_ref[...]).astype(o_ref.dtype)


def _rmsnorm(x, g, *, rows, out_dtype):
    m, d = x.shape
    return pl.pallas_call(
        _rmsnorm_kernel,
        out_shape=jax.ShapeDtypeStruct((m, d), out_dtype),
        grid=(m // rows,),
        in_specs=[pl.BlockSpec((rows, d), lambda i: (i, 0)),
                  pl.BlockSpec((1, d), lambda i: (0, 0))],
        out_specs=pl.BlockSpec((rows, d), lambda i: (i, 0)),
        compiler_params=_params(("parallel",)),
        name="rmsnorm",
    )(x, g.reshape(1, d))


def _final_norm_kernel(x_ref, g_ref, o_ref):
    x = x_ref[...]
    y = x * lax.rsqrt(jnp.mean(x * x, axis=-1, keepdims=True) + EPS)
    o_ref[...] = y * g_ref[...]


def _final_norm(h, g, *, batch, rows_per_batch, seq):
    d = h.shape[1]
    c = META_CHUNK
    nb = 8
    out = pl.pallas_call(
        _final_norm_kernel,
        out_shape=jax.ShapeDtypeStruct((batch, seq // c, c, d), F32),
        grid=(batch, seq // (c * nb)),
        in_specs=[pl.BlockSpec((pl.Element(1), pl.Element(nb), pl.Element(c), pl.Element(d)),
                               lambda b, j: (b, 1 + nb * j, 0, 0)),
                  pl.BlockSpec((1, d), lambda b, j: (0, 0))],
        out_specs=pl.BlockSpec((1, nb, c, d), lambda b, j: (b, j, 0, 0)),
        compiler_params=_params(("parallel", "parallel")),
        name="final_norm",
    )(h.reshape(batch, rows_per_batch // c, c, d), g.reshape(1, d))
    return out.reshape(batch, seq, d)


def _mm_kernel(a_ref, w_ref, o_ref):
    acc = jnp.dot(a_ref[...], w_ref[...].astype(BF16), preferred_element_type=F32)
    o_ref[...] = acc.astype(o_ref.dtype)


def _mm_res_kernel(a_ref, w_ref, r_ref, o_ref):
    acc = jnp.dot(a_ref[...], w_ref[...].astype(BF16), preferred_element_type=F32)
    o_ref[...] = (r_ref[...] + acc).astype(o_ref.dtype)


def _matmul(a, w, *, tm, tn, n, out_dtype, residual=None, name):
    m, k = a.shape
    in_specs = [pl.BlockSpec((tm, k), lambda i, j: (i, 0), pipeline_mode=pl.Buffered(1)),
                pl.BlockSpec((k, tn), lambda i, j: (0, j))]
    args = [a, w]
    kern = _mm_kernel
    if residual is not None:
        in_specs.append(pl.BlockSpec((tm, tn), lambda i, j: (i, j)))
        args.append(residual)
        kern = _mm_res_kernel
    return pl.pallas_call(
        kern,
        out_shape=jax.ShapeDtypeStruct((m, n), out_dtype),
        grid=(m // tm, n // tn),
        in_specs=in_specs,
        out_specs=pl.BlockSpec((tm, tn), lambda i, j: (i, j)),
        compiler_params=_params(("parallel", "parallel")),
        name=name,
    )(*args)


def _rope_table_kernel(pos_ref, invf_ref, cos_ref, sin_ref):
    ang = pos_ref[...].astype(F32) * invf_ref[...]
    cos_ref[...] = jnp.cos(ang)
    sin_ref[...] = jnp.sin(ang)


def _rope_tables(pos_col, inv_freq, *, rows):
    m = pos_col.shape[0]
    half = inv_freq.shape[1]
    out = jax.ShapeDtypeStruct((m, half), F32)
    return pl.pallas_call(
        _rope_table_kernel,
        out_shape=(out, out),
        grid=(m // rows,),
        in_specs=[pl.BlockSpec((rows, 1), lambda i: (i, 0)),
                  pl.BlockSpec((1, half), lambda i: (0, 0))],
        out_specs=(pl.BlockSpec((rows, half), lambda i: (i, 0)),
                   pl.BlockSpec((rows, half), lambda i: (i, 0))),
        compiler_params=_params(("parallel",)),
        name="rope_tables",
    )(pos_col, inv_freq)


def _ret_chunk(r0, n, lg, q_ref, k_ref, v_ref, o_ref, m_ref, cos_ref, sin_ref, g_ref, out_ref, s_ref):
    rows = pl.ds(r0, n)
    half = RET_DK // 2
    cos = cos_ref[rows, :]
    sin = sin_ref[rows, :]

    def rope(ref):
        x = ref[rows, :].astype(F32)
        x1, x2 = x[:, :half], x[:, half:]
        return jnp.concatenate([x1 * cos - x2 * sin, x2 * cos + x1 * sin], axis=-1)

    q = rope(q_ref)
    k = rope(k_ref) * (RET_DK ** -0.5)
    v = v_ref[rows, :]

    rel = (_iota((n, n), 0) - _iota((n, n), 1)).astype(F32)
    d_intra = jnp.where(rel >= 0, jnp.exp(lg * jnp.maximum(rel, 0.0)), 0.0)
    ridx = _iota((n, 1), 0).astype(F32)
    d_q = jnp.exp(lg * (ridx + 1.0))
    d_k = jnp.exp(lg * (n - 1.0 - ridx))
    d_chunk = jnp.exp(lg * float(n))

    qb = q.astype(BF16)
    scores = lax.dot_general(qb, k.astype(BF16), (((1,), (1,)), ((), ())),
                             preferred_element_type=F32) * d_intra
    inner = jnp.dot(scores.astype(BF16), v, preferred_element_type=F32)
    state = s_ref[...]
    cross = jnp.dot(qb, state.astype(BF16), preferred_element_type=F32) * d_q
    y = inner + cross
    upd = lax.dot_general((k * d_k).astype(BF16), v, (((0,), (0,)), ((), ())),
                          preferred_element_type=F32)
    s_ref[...] = state * d_chunk + upd

    mu = jnp.mean(y, axis=-1, keepdims=True)
    yc = y - mu
    var = jnp.mean(yc * yc, axis=-1, keepdims=True)
    yn = yc * lax.rsqrt(var + EPS) * g_ref[...]
    o = o_ref[rows, :].astype(F32)
    m = m_ref[rows, :].astype(F32)
    out_ref[rows, :] = (_sigmoid(m) * (yn * (o * _sigmoid(o)))).astype(out_ref.dtype)


def _ret_kernel(lg_ref, q_ref, k_ref, v_ref, o_ref, m_ref, cos_ref, sin_ref, g_ref, out_ref, s_ref):
    lg = lg_ref[0][:, :1]
    refs = (q_ref, k_ref, v_ref, o_ref, m_ref, cos_ref, sin_ref, g_ref, out_ref, s_ref)
    s_ref[...] = jnp.zeros_like(s_ref)
    _ret_chunk(0, META_CHUNK, lg, *refs)
    n_chunks = (q_ref.shape[0] - META_CHUNK) // RET_CHUNK

    def body(c, carry):
        r0 = pl.multiple_of(META_CHUNK + c * RET_CHUNK, META_CHUNK)
        _ret_chunk(r0, RET_CHUNK, lg, *refs)
        return carry

    lax.fori_loop(0, n_chunks, body, 0)


def _retention(proj, gates, cos, sin, g_ret, *, batch, rows):
    m = proj.shape[0]
    log_gamma = jnp.log1p(-jnp.exp2(-5.0 - jnp.arange(RET_HEADS, dtype=F32)))
    lg = jnp.broadcast_to(log_gamma[:, None, None], (RET_HEADS, 1, LANES))
    kq, kv = RET_QK // RET_DK, (2 * RET_QK) // RET_DV
    return pl.pallas_call(
        _ret_kernel,
        out_shape=jax.ShapeDtypeStruct((m, D_MODEL), BF16),
        grid=(batch, RET_HEADS),
        in_specs=[
            pl.BlockSpec((1, 1, LANES), lambda b, h: (h, 0, 0)),
            pl.BlockSpec((rows, RET_DK), lambda b, h: (b, h)),
            pl.BlockSpec((rows, RET_DK), lambda b, h: (b, kq + h)),
            pl.BlockSpec((rows, RET_DV), lambda b, h: (b, kv + h)),
            pl.BlockSpec((rows, RET_DV), lambda b, h: (b, kv + RET_HEADS + h)),
            pl.BlockSpec((rows, RET_DV), lambda b, h: (b, h)),
            pl.BlockSpec((rows, RET_DK // 2), lambda b, h: (b, 0)),
            pl.BlockSpec((rows, RET_DK // 2), lambda b, h: (b, 0)),
            pl.BlockSpec((1, RET_DV), lambda b, h: (0, h)),
        ],
        out_specs=pl.BlockSpec((rows, RET_DV), lambda b, h: (b, h)),
        scratch_shapes=[pltpu.VMEM((RET_DK, RET_DV), F32)],
        compiler_params=_params(("parallel", "parallel")),
        name="retention",
    )(lg, proj, proj, proj, proj, gates, cos, sin, g_ret.reshape(1, D_MODEL))


def _gla_chunk(r0, q_ref, k_ref, v_ref, o_ref, m_ref, z_ref, wgu_ref, bg_ref, g_ref, yr_ref,
               out_ref, s_ref):
    c_len, sub = GLA_CHUNK, GLA_SUB
    rows = pl.ds(r0, c_len)

    u = jnp.dot(z_ref[rows, :].astype(BF16), wgu_ref[...], preferred_element_type=F32) + bg_ref[...]
    log_a = (jnp.minimum(u, 0.0) - jnp.log1p(jnp.exp(-jnp.abs(u)))) * (1.0 / GLA_GATE_TAU)
    tri = (_iota((c_len, c_len), 0) >= _iota((c_len, c_len), 1)).astype(BF16)
    hi = log_a.astype(BF16)
    r1 = log_a - hi.astype(F32)
    mid = r1.astype(BF16)
    lo = (r1 - mid.astype(F32)).astype(BF16)
    b = (jnp.dot(tri, hi, preferred_element_type=F32)
         + jnp.dot(tri, mid, preferred_element_type=F32)
         + jnp.dot(tri, lo, preferred_element_type=F32))

    q = q_ref[rows, :].astype(F32) * (GLA_DK ** -0.5)
    k = k_ref[rows, :].astype(F32)
    v = v_ref[rows, :]

    state = s_ref[...]
    cross = jnp.dot((q * jnp.exp(b)).astype(BF16), state.astype(BF16), preferred_element_type=F32)

    key_row = _iota((c_len, 1), 0)
    lane = _iota((sub, c_len), 1)
    qrow = _iota((sub, c_len), 0)
    blocks = []
    for s in range(c_len // sub):
        sl = slice(sub * s, sub * (s + 1))
        bs, qs, ks = b[sl], q[sl], k[sl]
        acc = jnp.zeros((sub, c_len), F32)
        for j in range(sub):
            w = jnp.exp(jnp.minimum(bs - bs[j:j + 1, :], 0.0))
            col = jnp.sum(qs * w * ks[j:j + 1, :], axis=-1, keepdims=True)
            acc = jnp.where((lane == sub * s + j) & (qrow >= j), col, acc)
        if s > 0:
            b_ref_row = b[sub * s - 1:sub * s, :]
            qt = qs * jnp.exp(bs - b_ref_row)
            earlier = key_row < sub * s
            kt = jnp.where(earlier, k * jnp.exp(jnp.where(earlier, b_ref_row - b, 0.0)), 0.0)
            acc = acc + lax.dot_general(qt.astype(BF16), kt.astype(BF16), (((1,), (1,)), ((), ())),
                                        preferred_element_type=F32)
        blocks.append(acc)
    scores = jnp.concatenate(blocks, axis=0)
    y = jnp.dot(scores.astype(BF16), v, preferred_element_type=F32) + cross

    b_last = b[c_len - 1:c_len, :]
    kd = k * jnp.exp(b_last - b)
    upd = lax.dot_general(kd.astype(BF16), v, (((0,), (0,)), ((), ())), preferred_element_type=F32)
    decay_col = jnp.broadcast_to(jnp.exp(b_last), (LANES, GLA_DK)).T
    s_ref[...] = state * jnp.tile(decay_col, (1, GLA_DV // LANES)) + upd

    yn = y * lax.rsqrt(jnp.mean(y * y, axis=-1, keepdims=True) + EPS) * g_ref[...]
    o = o_ref[rows, :].astype(F32)
    m = m_ref[rows, :].astype(F32)
    merged = _sigmoid(m) * (yn * (o * _sigmoid(o))) + yr_ref[rows, :].astype(F32)
    out_ref[rows, :] = merged.astype(out_ref.dtype)


def _gla_kernel(q_ref, k_ref, v_ref, o_ref, m_ref, z_ref, wgu_ref, bg_ref, g_ref, yr_ref,
                out_ref, s_ref):
    @pl.when(pl.program_id(2) == 0)
    def _():
        s_ref[...] = jnp.zeros_like(s_ref)

    refs = (q_ref, k_ref, v_ref, o_ref, m_ref, z_ref, wgu_ref, bg_ref, g_ref, yr_ref, out_ref, s_ref)

    def body(c, carry):
        _gla_chunk(pl.multiple_of(c * GLA_CHUNK, GLA_CHUNK), *refs)
        return carry

    lax.fori_loop(0, q_ref.shape[0] // GLA_CHUNK, body, 0)


def _gla_merge(proj, gates, z, wgu, bg, g_gla, y_ret, *, batch, rows):
    m = proj.shape[0]
    bs = GLA_SEQ_BLOCK
    nsb = rows // bs
    cq = (2 * RET_QK + 2 * D_MODEL) // GLA_DK
    ck = cq + GLA_HEADS
    cv = (2 * RET_QK + 2 * D_MODEL + 2 * GLA_HEADS * GLA_DK) // GLA_DV
    co = cv + GLA_HEADS

    def row(b, j):
        return b * nsb + j

    return pl.pallas_call(
        _gla_kernel,
        out_shape=jax.ShapeDtypeStruct((m, D_MODEL), BF16),
        grid=(batch, GLA_HEADS, nsb),
        in_specs=[
            pl.BlockSpec((bs, GLA_DK), lambda b, h, j: (row(b, j), cq + h)),
            pl.BlockSpec((bs, GLA_DK), lambda b, h, j: (row(b, j), ck + h)),
            pl.BlockSpec((bs, GLA_DV), lambda b, h, j: (row(b, j), cv + h)),
            pl.BlockSpec((bs, GLA_DV), lambda b, h, j: (row(b, j), co + h)),
            pl.BlockSpec((bs, GLA_DV), lambda b, h, j: (row(b, j), GLA_HEADS + h)),
            pl.BlockSpec((bs, LANES), lambda b, h, j: (row(b, j), 0)),
            pl.BlockSpec((LANES, GLA_DK), lambda b, h, j: (0, h)),
            pl.BlockSpec((1, GLA_DK), lambda b, h, j: (0, h)),
            pl.BlockSpec((1, GLA_DV), lambda b, h, j: (0, h)),
            pl.BlockSpec((bs, GLA_DV), lambda b, h, j: (row(b, j), h)),
        ],
        out_specs=pl.BlockSpec((bs, GLA_DV), lambda b, h, j: (row(b, j), h)),
        scratch_shapes=[pltpu.VMEM((GLA_DK, GLA_DV), F32)],
        compiler_params=_params(("parallel", "parallel", "arbitrary")),
        name="gla_merge",
    )(proj, proj, proj, proj, gates, z, wgu, bg, g_gla.reshape(1, D_MODEL), y_ret)


def _ffn_in_kernel(a_ref, wu_ref, wg_ref, cw_ref, cb_ref, hid_ref, buf_ref):
    a = a_ref[...]
    up = jnp.dot(a, wu_ref[...].astype(BF16), preferred_element_type=F32)
    gate = jnp.dot(a, wg_ref[...].astype(BF16), preferred_element_type=F32)
    tm = up.shape[0]
    buf_ref[0:8, :] = jnp.zeros((8, up.shape[1]), F32)
    buf_ref[8:8 + tm, :] = up
    cw = cw_ref[...]
    c = (cb_ref[...] + cw[0:1, :] * buf_ref[6:6 + tm, :] + cw[1:2, :] * buf_ref[7:7 + tm, :]
         + cw[2:3, :] * up)
    hid_ref[...] = (c * _sigmoid(c) * gate).astype(hid_ref.dtype)


def _ffn_in(a, w_ffn_in, conv_w, conv_b, *, rows, tf):
    m, k = a.shape
    nf = D_FF // tf
    return pl.pallas_call(
        _ffn_in_kernel,
        out_shape=jax.ShapeDtypeStruct((m, D_FF), BF16),
        grid=(m // rows, nf),
        in_specs=[
            pl.BlockSpec((rows, k), lambda i, j: (i, 0), pipeline_mode=pl.Buffered(1)),
            pl.BlockSpec((k, tf), lambda i, j: (0, j)),
            pl.BlockSpec((k, tf), lambda i, j: (0, nf + j)),
            pl.BlockSpec((CONV_W, tf), lambda i, j: (0, j)),
            pl.BlockSpec((1, tf), lambda i, j: (0, j)),
        ],
        out_specs=pl.BlockSpec((rows, tf), lambda i, j: (i, j)),
        scratch_shapes=[pltpu.VMEM((rows + 8, tf), F32)],
        compiler_params=_params(("parallel", "parallel")),
        name="ffn_in_conv",
    )(a, w_ffn_in, w_ffn_in, conv_w, conv_b.reshape(1, D_FF))


def kernel(x, positions, meta_tokens, attn_norm, w_in, w_gate_up, b_gate, ret_norm, gla_norm,
           w_out, ffn_norm, w_ffn_in, conv_w, conv_b, w_ffn_out, final_norm):
    batch, seq, d = x.shape
    rows = PAD_ROWS + N_META + seq
    m = batch * rows

    meta = jnp.broadcast_to(meta_tokens[None].astype(x.dtype), (batch, N_META, d))
    h0 = jnp.concatenate([jnp.zeros((batch, PAD_ROWS, d), x.dtype), meta, x], axis=1).reshape(m, d)
    pos = jnp.concatenate([
        jnp.zeros((batch, PAD_ROWS), jnp.int32),
        jnp.broadcast_to(jnp.arange(N_META, dtype=jnp.int32), (batch, N_META)),
        positions.astype(jnp.int32) + N_META], axis=1).reshape(m, 1)
    half = RET_DK // 2
    inv_freq = (ROPE_BASE ** (-jnp.arange(half, dtype=F32) / half)).reshape(1, half)

    w_in0 = w_in[0]
    w_z = jnp.pad(w_in0[:, N_MAIN:N_MAIN + GLA_RANK], ((0, 0), (0, LANES - GLA_RANK)))
    w_gates = w_in0[:, N_MAIN + GLA_RANK:]
    wgu = jnp.pad(w_gate_up[0], ((0, LANES - GLA_RANK), (0, 0))).astype(BF16)
    w_ffn_out_b = w_ffn_out[0].astype(BF16)

    cos, sin = _rope_tables(pos, inv_freq, rows=rows)

    hn = _rmsnorm(h0, attn_norm[0], rows=528, out_dtype=BF16)
    proj = _matmul(hn, w_in0, tm=rows, tn=512, n=N_MAIN, out_dtype=BF16, name="in_proj")
    gates = _matmul(hn, w_gates, tm=rows, tn=512, n=2 * d, out_dtype=BF16, name="in_proj_gates")
    z = _matmul(hn, w_z, tm=rows, tn=LANES, n=LANES, out_dtype=F32, name="in_proj_z")

    y_ret = _retention(proj, gates, cos, sin, ret_norm[0], batch=batch, rows=rows)
    merged = _gla_merge(proj, gates, z, wgu, b_gate[0].reshape(1, -1), gla_norm[0], y_ret,
                        batch=batch, rows=rows)

    h1 = _matmul(merged, w_out[0], tm=rows, tn=256, n=d, out_dtype=F32, residual=h0,
                 name="out_proj")
    h1n = _rmsnorm(h1, ffn_norm[0], rows=528, out_dtype=BF16)
    hidden = _ffn_in(h1n, w_ffn_in[0], conv_w[0], conv_b[0], rows=rows, tf=256)
    h2 = _matmul(hidden, w_ffn_out_b, tm=1056, tn=256, n=d, out_dtype=F32, residual=h1,
                 name="ffn_out")
    return _final_norm(h2, final_norm, batch=batch, rows_per_batch=rows, seq=seq)
```

```python
import jax
import jax.numpy as jnp
from jax import lax
from jax.experimental import pallas as pl
from jax.experimental.pallas import tpu as pltpu

F32 = jnp.float32
BF16 = jnp.bfloat16

D_MODEL = 4096
N_META = 16
PAD_ROWS = 48
RET_HEADS, RET_DK, RET_DV = 8, 256, 512
GLA_HEADS, GLA_DK, GLA_DV = 4, 512, 1024
GLA_RANK = 16
GLA_GATE_TAU = 16.0
D_FF = 11008
CONV_W = 3
ROPE_BASE = 10000.0
EPS = 1e-6
RET_QK = RET_HEADS * RET_DK
GLA_QK = GLA_HEADS * GLA_DK
COL_Q_R, COL_K_R = 0, RET_QK
COL_V_R, COL_O_R = 2 * RET_QK, 2 * RET_QK + D_MODEL
COL_Q_G = 2 * RET_QK + 2 * D_MODEL
COL_K_G, COL_V_G = COL_Q_G + GLA_QK, COL_Q_G + 2 * GLA_QK
COL_O_G = COL_V_G + D_MODEL
N_MAIN = COL_O_G + D_MODEL
COL_M_R, COL_M_G = N_MAIN, N_MAIN + D_MODEL
N_PROJ = N_MAIN + 2 * D_MODEL

VMEM_LIMIT_V7X = 56 * 1024 * 1024
LANES = 128
META_CHUNK = 64
BIG_CHUNK = 256
GLA_EXACT_CHUNK = 64
GLA_SUB = 16
GLA_SAFE_SPAN = 60.0


def _params(sem):
    return pltpu.CompilerParams(dimension_semantics=sem, vmem_limit_bytes=VMEM_LIMIT_V7X)


def _sigmoid(x):
    return 1.0 / (1.0 + jnp.exp(-x))


def _iota(shape, dim):
    return lax.broadcasted_iota(jnp.int32, shape, dim)


def _dot(a, b):
    return jnp.dot(a, b, preferred_element_type=F32)


def _dot_nt(a, b):
    return lax.dot_general(a, b, (((1,), (1,)), ((), ())), preferred_element_type=F32)


def _dot_tn(a, b):
    return lax.dot_general(a, b, (((0,), (0,)), ((), ())), preferred_element_type=F32)


def _rmsnorm_kernel(x_ref, g_ref, o_ref):
    x = x_ref[...]
    y = x * lax.rsqrt(jnp.mean(x * x, axis=-1, keepdims=True) + EPS)
    o_ref[...] = (y * g_ref[...]).astype(o_ref.dtype)


def _rmsnorm(x, g, *, rows, out_dtype):
    m, d = x.shape
    return pl.pallas_call(
        _rmsnorm_kernel,
        out_shape=jax.ShapeDtypeStruct((m, d), out_dtype),
        grid=(m // rows,),
        in_specs=[pl.BlockSpec((rows, d), lambda i: (i, 0)),
                  pl.BlockSpec((1, d), lambda i: (0, 0))],
        out_specs=pl.BlockSpec((rows, d), lambda i: (i, 0)),
        compiler_params=_params(("parallel",)),
        name="rmsnorm",
    )(x, g.reshape(1, d))


def _final_norm_kernel(x_ref, g_ref, o_ref):
    x = x_ref[...]
    y = x * lax.rsqrt(jnp.mean(x * x, axis=-1, keepdims=True) + EPS)
    o_ref[...] = y * g_ref[...]


def _final_norm(h, g, *, batch, rows_per_batch, seq):
    d = h.shape[1]
    c = META_CHUNK
    nb = 8
    out = pl.pallas_call(
        _final_norm_kernel,
        out_shape=jax.ShapeDtypeStruct((batch, seq // c, c, d), F32),
        grid=(batch, seq // (c * nb)),
        in_specs=[pl.BlockSpec((pl.Element(1), pl.Element(nb), pl.Element(c), pl.Element(d)),
                               lambda b, j: (b, 1 + nb * j, 0, 0)),
                  pl.BlockSpec((1, d), lambda b, j: (0, 0))],
        out_specs=pl.BlockSpec((1, nb, c, d), lambda b, j: (b, j, 0, 0)),
        compiler_params=_params(("parallel", "parallel")),
        name="final_norm",
    )(h.reshape(batch, rows_per_batch // c, c, d), g.reshape(1, d))
    return out.reshape(batch, seq, d)


def _mm_res_kernel(a_ref, w_ref, r_ref, o_ref):
    o_ref[...] = (r_ref[...] + _dot(a_ref[...], w_ref[...].astype(BF16))).astype(o_ref.dtype)


def _matmul_res(a, w, residual, *, tm, tn, name):
    m, k = a.shape
    n = w.shape[1]
    return pl.pallas_call(
        _mm_res_kernel,
        out_shape=jax.ShapeDtypeStruct((m, n), F32),
        grid=(m // tm, n // tn),
        in_specs=[pl.BlockSpec((tm, k), lambda i, j: (i, 0), pipeline_mode=pl.Buffered(1)),
                  pl.BlockSpec((k, tn), lambda i, j: (0, j)),
                  pl.BlockSpec((tm, tn), lambda i, j: (i, j))],
        out_specs=pl.BlockSpec((tm, tn), lambda i, j: (i, j)),
        compiler_params=_params(("parallel", "parallel")),
        name=name,
    )(a, w, residual)


def _mm_nt_kernel(a_ref, wt_ref, o_ref):
    o_ref[...] = _dot_nt(a_ref[...], wt_ref[...].astype(BF16)).astype(o_ref.dtype)


def _matmul_nt(a, wt, *, tm, tn, n, row_start, out_dtype, name):
    m, k = a.shape
    return pl.pallas_call(
        _mm_nt_kernel,
        out_shape=jax.ShapeDtypeStruct((m, n), out_dtype),
        grid=(m // tm, n // tn),
        in_specs=[pl.BlockSpec((tm, k), lambda i, j: (i, 0), pipeline_mode=pl.Buffered(1)),
                  pl.BlockSpec((pl.Element(tn), pl.Element(k)), lambda i, j: (row_start(j), 0))],
        out_specs=pl.BlockSpec((tm, tn), lambda i, j: (i, j)),
        compiler_params=_params(("parallel", "parallel")),
        name=name,
    )(a, wt)


def _rope_table_kernel(pos_ref, invf_ref, cos_ref, sin_ref):
    ang = pos_ref[...].astype(F32) * invf_ref[...]
    cos_ref[...] = jnp.cos(ang)
    sin_ref[...] = jnp.sin(ang)


def _rope_tables(pos_col, inv_freq, *, rows):
    m = pos_col.shape[0]
    half = inv_freq.shape[1]
    out = jax.ShapeDtypeStruct((m, half), F32)
    return pl.pallas_call(
        _rope_table_kernel,
        out_shape=(out, out),
        grid=(m // rows,),
        in_specs=[pl.BlockSpec((rows, 1), lambda i: (i, 0)),
                  pl.BlockSpec((1, half), lambda i: (0, 0))],
        out_specs=(pl.BlockSpec((rows, half), lambda i: (i, 0)),
                   pl.BlockSpec((rows, half), lambda i: (i, 0))),
        compiler_params=_params(("parallel",)),
        name="rope_tables",
    )(pos_col, inv_freq)


def _gla_log_decay(rows, n, z_ref, wgu_ref, bg_ref):
    u = _dot(z_ref[rows, :].astype(BF16), wgu_ref[...]) + bg_ref[...]
    log_a = (jnp.minimum(u, 0.0) - jnp.log1p(jnp.exp(-jnp.abs(u)))) * (1.0 / GLA_GATE_TAU)
    tri = (_iota((n, n), 0) >= _iota((n, n), 1)).astype(BF16)
    hi = log_a.astype(BF16)
    r1 = log_a - hi.astype(F32)
    mid = r1.astype(BF16)
    lo = (r1 - mid.astype(F32)).astype(BF16)
    return _dot(tri, hi) + _dot(tri, mid) + _dot(tri, lo)


def _gla_finish(rows, y, q, k, v, b, g_ref, out_ref, s_ref):
    n = b.shape[0]
    b_last = b[n - 1:n, :]
    kd = k * jnp.exp(b_last - b)
    upd = _dot_tn(kd.astype(BF16), v)
    decay_col = jnp.broadcast_to(jnp.exp(b_last), (LANES, GLA_DK)).T
    s_ref[...] = s_ref[...] * jnp.tile(decay_col, (1, GLA_DV // LANES)) + upd
    yn = y * lax.rsqrt(jnp.mean(y * y, axis=-1, keepdims=True) + EPS) * g_ref[...]
    out_ref[rows, :] = yn.astype(out_ref.dtype)


def _gla_chunk_direct(r0, n, b, q_ref, k_ref, v_ref, g_ref, out_ref, s_ref):
    rows = pl.ds(r0, n)
    q = q_ref[rows, :].astype(F32) * (GLA_DK ** -0.5)
    k = k_ref[rows, :].astype(F32)
    v = v_ref[rows, :]
    qe = (q * jnp.exp(b)).astype(BF16)
    cross = _dot(qe, s_ref[...].astype(BF16))
    scores = _dot_nt(qe, (k * jnp.exp(-b)).astype(BF16))
    causal = _iota((n, n), 0) >= _iota((n, n), 1)
    y = _dot(jnp.where(causal, scores, 0.0).astype(BF16), v) + cross
    _gla_finish(rows, y, q, k, v, b, g_ref, out_ref, s_ref)


def _gla_chunk_exact(r0, q_ref, k_ref, v_ref, z_ref, wgu_ref, bg_ref, g_ref, out_ref, s_ref):
    c_len, sub = GLA_EXACT_CHUNK, GLA_SUB
    rows = pl.ds(r0, c_len)
    b = _gla_log_decay(rows, c_len, z_ref, wgu_ref, bg_ref)
    q = q_ref[rows, :].astype(F32) * (GLA_DK ** -0.5)
    k = k_ref[rows, :].astype(F32)
    v = v_ref[rows, :]
    cross = _dot((q * jnp.exp(b)).astype(BF16), s_ref[...].astype(BF16))

    key_row = _iota((c_len, 1), 0)
    lane = _iota((sub, c_len), 1)
    qrow = _iota((sub, c_len), 0)
    blocks = []
    for s in range(c_len // sub):
        sl = slice(sub * s, sub * (s + 1))
        bs, qs, ks = b[sl], q[sl], k[sl]
        acc = jnp.zeros((sub, c_len), F32)
        for j in range(sub):
            w = jnp.exp(jnp.minimum(bs - bs[j:j + 1, :], 0.0))
            col = jnp.sum(qs * w * ks[j:j + 1, :], axis=-1, keepdims=True)
            acc = jnp.where((lane == sub * s + j) & (qrow >= j), col, acc)
        if s > 0:
            b_ref_row = b[sub * s - 1:sub * s, :]
            qt = qs * jnp.exp(bs - b_ref_row)
            earlier = key_row < sub * s
            kt = jnp.where(earlier, k * jnp.exp(jnp.where(earlier, b_ref_row - b, 0.0)), 0.0)
            acc = acc + _dot_nt(qt.astype(BF16), kt.astype(BF16))
        blocks.append(acc)
    scores = jnp.concatenate(blocks, axis=0)
    y = _dot(scores.astype(BF16), v) + cross
    _gla_finish(rows, y, q, k, v, b, g_ref, out_ref, s_ref)


def _gla_chunk(r0, n, q_ref, k_ref, v_ref, z_ref, wgu_ref, bg_ref, g_ref, out_ref, s_ref):
    b = _gla_log_decay(pl.ds(r0, n), n, z_ref, wgu_ref, bg_ref)
    span = jnp.max(-b[n - 1:n, :])
    small = span <= GLA_SAFE_SPAN

    @pl.when(small)
    def _():
        _gla_chunk_direct(r0, n, b, q_ref, k_ref, v_ref, g_ref, out_ref, s_ref)

    @pl.when(jnp.logical_not(small))
    def _():
        def body(i, carry):
            _gla_chunk_exact(pl.multiple_of(r0 + i * GLA_EXACT_CHUNK, GLA_EXACT_CHUNK),
                             q_ref, k_ref, v_ref, z_ref, wgu_ref, bg_ref, g_ref, out_ref, s_ref)
            return carry

        lax.fori_loop(0, n // GLA_EXACT_CHUNK, body, 0)


def _gla_kernel(q_ref, k_ref, v_ref, z_ref, wgu_ref, bg_ref, g_ref, out_ref, s_ref):
    refs = (q_ref, k_ref, v_ref, z_ref, wgu_ref, bg_ref, g_ref, out_ref, s_ref)
    s_ref[...] = jnp.zeros_like(s_ref)
    _gla_chunk(0, META_CHUNK, *refs)

    def body(c, carry):
        _gla_chunk(pl.multiple_of(META_CHUNK + c * BIG_CHUNK, META_CHUNK), BIG_CHUNK, *refs)
        return carry

    lax.fori_loop(0, (q_ref.shape[0] - META_CHUNK) // BIG_CHUNK, body, 0)


def _gla(proj, z, wgu, bg, g_gla, *, batch, rows):
    m = proj.shape[0]
    return pl.pallas_call(
        _gla_kernel,
        out_shape=jax.ShapeDtypeStruct((m, D_MODEL), BF16),
        grid=(batch, GLA_HEADS),
        in_specs=[
            pl.BlockSpec((rows, GLA_DK), lambda b, h: (b, COL_Q_G // GLA_DK + h)),
            pl.BlockSpec((rows, GLA_DK), lambda b, h: (b, COL_K_G // GLA_DK + h)),
            pl.BlockSpec((rows, GLA_DV), lambda b, h: (b, COL_V_G // GLA_DV + h)),
            pl.BlockSpec((rows, GLA_RANK), lambda b, h: (b, 0)),
            pl.BlockSpec((GLA_RANK, GLA_DK), lambda b, h: (0, h)),
            pl.BlockSpec((1, GLA_DK), lambda b, h: (0, h)),
            pl.BlockSpec((1, GLA_DV), lambda b, h: (0, h)),
        ],
        out_specs=pl.BlockSpec((rows, GLA_DV), lambda b, h: (b, h)),
        scratch_shapes=[pltpu.VMEM((GLA_DK, GLA_DV), F32)],
        compiler_params=_params(("parallel", "parallel")),
        name="gla",
    )(proj, proj, proj, z, wgu, bg, g_gla.reshape(1, D_MODEL))


def _ret_chunk(r0, n, lg, q_ref, k_ref, v_ref, o_ref, m_ref, cos_ref, sin_ref, g_ref,
               yg_ref, og_ref, mg_ref, out_ref, s_ref):
    rows = pl.ds(r0, n)
    half = RET_DK // 2
    cos = cos_ref[rows, :]
    sin = sin_ref[rows, :]

    def rope(ref):
        x = ref[rows, :].astype(F32)
        x1, x2 = x[:, :half], x[:, half:]
        return jnp.concatenate([x1 * cos - x2 * sin, x2 * cos + x1 * sin], axis=-1)

    q = rope(q_ref)
    k = rope(k_ref) * (RET_DK ** -0.5)
    v = v_ref[rows, :]

    rel = (_iota((n, n), 0) - _iota((n, n), 1)).astype(F32)
    d_intra = jnp.where(rel >= 0, jnp.exp(lg * jnp.maximum(rel, 0.0)), 0.0)
    ridx = _iota((n, 1), 0).astype(F32)
    d_q = jnp.exp(lg * (ridx + 1.0))
    d_k = jnp.exp(lg * (n - 1.0 - ridx))
    d_chunk = jnp.exp(lg * float(n))

    qb = q.astype(BF16)
    scores = _dot_nt(qb, k.astype(BF16)) * d_intra
    state = s_ref[...]
    y = _dot(scores.astype(BF16), v) + _dot(qb, state.astype(BF16)) * d_q
    s_ref[...] = state * d_chunk + _dot_tn((k * d_k).astype(BF16), v)

    mu = jnp.mean(y, axis=-1, keepdims=True)
    yc = y - mu
    var = jnp.mean(yc * yc, axis=-1, keepdims=True)
    yn = yc * lax.rsqrt(var + EPS) * g_ref[...]

    def gated(branch, o_gate_ref, m_gate_ref):
        o = o_gate_ref[rows, :].astype(F32)
        return _sigmoid(m_gate_ref[rows, :].astype(F32)) * (branch * (o * _sigmoid(o)))

    merged = gated(yn, o_ref, m_ref) + gated(yg_ref[rows, :].astype(F32), og_ref, mg_ref)
    out_ref[rows, :] = merged.astype(out_ref.dtype)


def _ret_kernel(lg_ref, q_ref, k_ref, v_ref, o_ref, m_ref, cos_ref, sin_ref, g_ref,
                yg_ref, og_ref, mg_ref, out_ref, s_ref):
    lg = lg_ref[0][:, :1]
    refs = (q_ref, k_ref, v_ref, o_ref, m_ref, cos_ref, sin_ref, g_ref, yg_ref, og_ref, mg_ref,
            out_ref, s_ref)
    s_ref[...] = jnp.zeros_like(s_ref)
    _ret_chunk(0, META_CHUNK, lg, *refs)

    def body(c, carry):
        r0 = pl.multiple_of(META_CHUNK + c * BIG_CHUNK, META_CHUNK)
        _ret_chunk(r0, BIG_CHUNK, lg, *refs)
        return carry

    lax.fori_loop(0, (q_ref.shape[0] - META_CHUNK) // BIG_CHUNK, body, 0)


def _retention_merge(proj, y_gla, cos, sin, g_ret, *, batch, rows):
    m = proj.shape[0]
    log_gamma = jnp.log1p(-jnp.exp2(-5.0 - jnp.arange(RET_HEADS, dtype=F32)))
    lg = jnp.broadcast_to(log_gamma[:, None, None], (RET_HEADS, 1, LANES))

    def cols(start, width):
        return lambda b, h: (b, start // width + h)

    dk, dv = RET_DK, RET_DV
    return pl.pallas_call(
        _ret_kernel,
        out_shape=jax.ShapeDtypeStruct((m, D_MODEL), BF16),
        grid=(batch, RET_HEADS),
        in_specs=[
            pl.BlockSpec((1, 1, LANES), lambda b, h: (h, 0, 0)),
            pl.BlockSpec((rows, dk), cols(COL_Q_R, dk)),
            pl.BlockSpec((rows, dk), cols(COL_K_R, dk)),
            pl.BlockSpec((rows, dv), cols(COL_V_R, dv)),
            pl.BlockSpec((rows, dv), cols(COL_O_R, dv)),
            pl.BlockSpec((rows, dv), cols(COL_M_R, dv)),
            pl.BlockSpec((rows, dk // 2), lambda b, h: (b, 0)),
            pl.BlockSpec((rows, dk // 2), lambda b, h: (b, 0)),
            pl.BlockSpec((1, dv), lambda b, h: (0, h)),
            pl.BlockSpec((rows, dv), lambda b, h: (b, h)),
            pl.BlockSpec((rows, dv), cols(COL_O_G, dv)),
            pl.BlockSpec((rows, dv), cols(COL_M_G, dv)),
        ],
        out_specs=pl.BlockSpec((rows, dv), lambda b, h: (b, h)),
        scratch_shapes=[pltpu.VMEM((dk, dv), F32)],
        compiler_params=_params(("parallel", "parallel")),
        name="retention_merge",
    )(lg, proj, proj, proj, proj, proj, cos, sin, g_ret.reshape(1, D_MODEL), y_gla, proj, proj)


def _ffn_in_kernel(a_ref, wu_ref, wg_ref, cw_ref, cb_ref, hid_ref, buf_ref):
    a = a_ref[...]
    up = _dot(a, wu_ref[...].astype(BF16))
    gate = _dot(a, wg_ref[...].astype(BF16))
    tm = up.shape[0]
    buf_ref[0:8, :] = jnp.zeros((8, up.shape[1]), F32)
    buf_ref[8:8 + tm, :] = up
    cw = cw_ref[...]
    c = (cb_ref[...] + cw[0:1, :] * buf_ref[6:6 + tm, :] + cw[1:2, :] * buf_ref[7:7 + tm, :]
         + cw[2:3, :] * up)
    hid_ref[...] = (c * _sigmoid(c) * gate).astype(hid_ref.dtype)


def _ffn_in(a, w_ffn_in, conv_w, conv_b, *, rows, tf):
    m, k = a.shape
    nf = D_FF // tf
    return pl.pallas_call(
        _ffn_in_kernel,
        out_shape=jax.ShapeDtypeStruct((m, D_FF), BF16),
        grid=(m // rows, nf),
        in_specs=[
            pl.BlockSpec((rows, k), lambda i, j: (i, 0), pipeline_mode=pl.Buffered(1)),
            pl.BlockSpec((k, tf), lambda i, j: (0, j)),
            pl.BlockSpec((k, tf), lambda i, j: (0, nf + j)),
            pl.BlockSpec((CONV_W, tf), lambda i, j: (0, j)),
            pl.BlockSpec((1, tf), lambda i, j: (0, j)),
        ],
        out_specs=pl.BlockSpec((rows, tf), lambda i, j: (i, j)),
        scratch_shapes=[pltpu.VMEM((rows + 8, tf), F32)],
        compiler_params=_params(("parallel", "parallel")),
        name="ffn_in_conv",
    )(a, w_ffn_in, w_ffn_in, conv_w, conv_b.reshape(1, D_FF))


def kernel(x, positions, meta_tokens, attn_norm, w_in, w_gate_up, b_gate, ret_norm, gla_norm,
           w_out, ffn_norm, w_ffn_in, conv_w, conv_b, w_ffn_out, final_norm):
    batch, seq, d = x.shape
    rows = PAD_ROWS + N_META + seq
    m = batch * rows

    meta = jnp.broadcast_to(meta_tokens[None].astype(x.dtype), (batch, N_META, d))
    h0 = jnp.concatenate([jnp.zeros((batch, PAD_ROWS, d), x.dtype), meta, x], axis=1).reshape(m, d)
    pos = jnp.concatenate([
        jnp.zeros((batch, PAD_ROWS), jnp.int32),
        jnp.broadcast_to(jnp.arange(N_META, dtype=jnp.int32), (batch, N_META)),
        positions.astype(jnp.int32) + N_META], axis=1).reshape(m, 1)
    half = RET_DK // 2
    inv_freq = (ROPE_BASE ** (-jnp.arange(half, dtype=F32) / half)).reshape(1, half)

    w_in_t = jnp.swapaxes(w_in, 1, 2)[0]
    w_ffn_out_b = w_ffn_out[0].astype(BF16)

    cos, sin = _rope_tables(pos, inv_freq, rows=rows)

    hn = _rmsnorm(h0, attn_norm[0], rows=528, out_dtype=BF16)
    tn = 512
    proj = _matmul_nt(hn, w_in_t, tm=rows, tn=tn, n=N_PROJ, out_dtype=BF16, name="in_proj",
                      row_start=lambda j: pl.multiple_of(
                          j * tn + jnp.where(j * tn >= N_MAIN, GLA_RANK, 0), GLA_RANK))
    z = _matmul_nt(hn, w_in_t, tm=rows, tn=GLA_RANK, n=GLA_RANK, out_dtype=F32, name="in_proj_z",
                   row_start=lambda j: N_MAIN)

    y_gla = _gla(proj, z, w_gate_up[0].astype(BF16), b_gate[0].reshape(1, -1), gla_norm[0],
                 batch=batch, rows=rows)
    merged = _retention_merge(proj, y_gla, cos, sin, ret_norm[0], batch=batch, rows=rows)

    h1 = _matmul_res(merged, w_out[0], h0, tm=rows, tn=256, name="out_proj")
    h1n = _rmsnorm(h1, ffn_norm[0], rows=528, out_dtype=BF16)
    hidden = _ffn_in(h1n, w_ffn_in[0], conv_w[0], conv_b[0], rows=rows, tf=256)
    h2 = _matmul_res(hidden, w_ffn_out_b, h1, tm=1056, tn=256, name="ffn_out")
    return _final_norm(h2, final_norm, batch=batch, rows_per_batch=rows, seq=seq)
```

```python
import jax
import jax.numpy as jnp
from jax import lax
from jax.experimental import pallas as pl
from jax.experimental.pallas import tpu as pltpu

F32 = jnp.float32
BF16 = jnp.bfloat16

D_MODEL = 4096
N_META = 16
RET_HEADS, RET_DK, RET_DV = 8, 256, 512
GLA_HEADS, GLA_DK, GLA_DV = 4, 512, 1024
GLA_RANK = 16
GLA_GATE_TAU = 16.0
D_FF = 11008
CONV_W = 3
ROPE_BASE = 10000.0
EPS = 1e-6
RET_QK = RET_HEADS * RET_DK
GLA_QK = GLA_HEADS * GLA_DK
COL_Q_R, COL_K_R = 0, RET_QK
COL_V_R, COL_O_R = 2 * RET_QK, 2 * RET_QK + D_MODEL
COL_Q_G = 2 * RET_QK + 2 * D_MODEL
COL_K_G, COL_V_G = COL_Q_G + GLA_QK, COL_Q_G + 2 * GLA_QK
COL_O_G = COL_V_G + D_MODEL
N_MAIN = COL_O_G + D_MODEL
COL_M_R, COL_M_G = N_MAIN, N_MAIN + D_MODEL
N_PROJ = N_MAIN + 2 * D_MODEL

VMEM_LIMIT_V7X = 56 * 1024 * 1024
LANES = 128
BIG_CHUNK = 256
FFN_ROW_CHUNK = 256
GLA_EXACT_CHUNK = 64
GLA_SUB = 16
GLA_SAFE_SPAN = 60.0


def _params(sem):
    return pltpu.CompilerParams(dimension_semantics=sem, vmem_limit_bytes=VMEM_LIMIT_V7X)


def _sigmoid(x):
    return 1.0 / (1.0 + jnp.exp(-x))


def _iota(shape, dim):
    return lax.broadcasted_iota(jnp.int32, shape, dim)


def _dot(a, b):
    return jnp.dot(a, b, preferred_element_type=F32)


def _dot_nt(a, b):
    return lax.dot_general(a, b, (((1,), (1,)), ((), ())), preferred_element_type=F32)


def _dot_tn(a, b):
    return lax.dot_general(a, b, (((0,), (0,)), ((), ())), preferred_element_type=F32)


def _rmsnorm_kernel(x_ref, g_ref, o_ref):
    x = x_ref[...]
    y = x * lax.rsqrt(jnp.mean(x * x, axis=-1, keepdims=True) + EPS)
    o_ref[...] = (y * g_ref[...]).astype(o_ref.dtype)


def _rmsnorm(x, g, *, rows, out_dtype):
    m, d = x.shape
    return pl.pallas_call(
        _rmsnorm_kernel,
        out_shape=jax.ShapeDtypeStruct((m, d), out_dtype),
        grid=(m // rows,),
        in_specs=[pl.BlockSpec((rows, d), lambda i: (i, 0)),
                  pl.BlockSpec((1, d), lambda i: (0, 0))],
        out_specs=pl.BlockSpec((rows, d), lambda i: (i, 0)),
        compiler_params=_params(("parallel",)),
        name="rmsnorm",
    )(x, g.reshape(1, d))


def _final_norm_kernel(x_ref, g_ref, o_ref):
    x = x_ref[...]
    y = x * lax.rsqrt(jnp.mean(x * x, axis=-1, keepdims=True) + EPS)
    o_ref[...] = y * g_ref[...]


def _final_norm(h, g, *, batch, rows_per_batch, seq, rows):
    d = h.shape[1]
    lead = rows_per_batch - seq
    return pl.pallas_call(
        _final_norm_kernel,
        out_shape=jax.ShapeDtypeStruct((batch, seq, d), F32),
        grid=(batch, seq // rows),
        in_specs=[pl.BlockSpec((pl.Element(1), pl.Element(rows), pl.Element(d)),
                               lambda b, j: (b, pl.multiple_of(lead + rows * j, N_META), 0)),
                  pl.BlockSpec((1, d), lambda b, j: (0, 0))],
        out_specs=pl.BlockSpec((1, rows, d), lambda b, j: (b, j, 0)),
        compiler_params=_params(("parallel", "parallel")),
        name="final_norm",
    )(h.reshape(batch, rows_per_batch, d), g.reshape(1, d))


def _mm_res_kernel(a_ref, w_ref, r_ref, o_ref):
    o_ref[...] = (r_ref[...] + _dot(a_ref[...], w_ref[...].astype(BF16))).astype(o_ref.dtype)


def _matmul_res(a, w, residual, *, tm, tn, a_buffers, name):
    m, k = a.shape
    n = w.shape[1]
    return pl.pallas_call(
        _mm_res_kernel,
        out_shape=jax.ShapeDtypeStruct((m, n), F32),
        grid=(m // tm, n // tn),
        in_specs=[pl.BlockSpec((tm, k), lambda i, j: (i, 0), pipeline_mode=pl.Buffered(a_buffers)),
                  pl.BlockSpec((k, tn), lambda i, j: (0, j)),
                  pl.BlockSpec((tm, tn), lambda i, j: (i, j))],
        out_specs=pl.BlockSpec((tm, tn), lambda i, j: (i, j)),
        compiler_params=_params(("parallel", "parallel")),
        name=name,
    )(a, w, residual)


def _in_proj_kernel(a_ref, wt_ref, wz_ref, o_ref, z_ref):
    o_ref[...] = _dot_nt(a_ref[...], wt_ref[...].astype(BF16)).astype(o_ref.dtype)

    @pl.when(pl.program_id(1) == 0)
    def _():
        z_ref[...] = _dot_nt(a_ref[...], wz_ref[...].astype(BF16))


def _in_proj(a, w_in_t, *, tm, tn):
    m, k = a.shape

    def row_start(j):
        return pl.multiple_of(j * tn + jnp.where(j * tn >= N_MAIN, GLA_RANK, 0), GLA_RANK)

    return pl.pallas_call(
        _in_proj_kernel,
        out_shape=(jax.ShapeDtypeStruct((m, N_PROJ), BF16), jax.ShapeDtypeStruct((m, GLA_RANK), F32)),
        grid=(m // tm, N_PROJ // tn),
        in_specs=[pl.BlockSpec((tm, k), lambda i, j: (i, 0), pipeline_mode=pl.Buffered(1)),
                  pl.BlockSpec((pl.Element(tn), pl.Element(k)), lambda i, j: (row_start(j), 0)),
                  pl.BlockSpec((GLA_RANK, k), lambda i, j: (N_MAIN // GLA_RANK, 0))],
        out_specs=(pl.BlockSpec((tm, tn), lambda i, j: (i, j)),
                   pl.BlockSpec((tm, GLA_RANK), lambda i, j: (i, 0))),
        compiler_params=_params(("parallel", "arbitrary")),
        name="in_proj",
    )(a, w_in_t, w_in_t)


def _rope_table_kernel(pos_ref, invf_ref, cos_ref, sin_ref):
    ang = pos_ref[...].astype(F32) * invf_ref[...]
    cos_ref[...] = jnp.cos(ang)
    sin_ref[...] = jnp.sin(ang)


def _rope_tables(pos_col, inv_freq, *, rows):
    m = pos_col.shape[0]
    half = inv_freq.shape[1]
    out = jax.ShapeDtypeStruct((m, half), F32)
    return pl.pallas_call(
        _rope_table_kernel,
        out_shape=(out, out),
        grid=(m // rows,),
        in_specs=[pl.BlockSpec((rows, 1), lambda i: (i, 0)),
                  pl.BlockSpec((1, half), lambda i: (0, 0))],
        out_specs=(pl.BlockSpec((rows, half), lambda i: (i, 0)),
                   pl.BlockSpec((rows, half), lambda i: (i, 0))),
        compiler_params=_params(("parallel",)),
        name="rope_tables",
    )(pos_col, inv_freq)


def _gla_log_decay(rows, n, z_ref, wgu_ref, bg_ref):
    u = _dot(z_ref[rows, :].astype(BF16), wgu_ref[...]) + bg_ref[...]
    log_a = (jnp.minimum(u, 0.0) - jnp.log1p(jnp.exp(-jnp.abs(u)))) * (1.0 / GLA_GATE_TAU)
    tri = (_iota((n, n), 0) >= _iota((n, n), 1)).astype(BF16)
    hi = log_a.astype(BF16)
    r1 = log_a - hi.astype(F32)
    mid = r1.astype(BF16)
    lo = (r1 - mid.astype(F32)).astype(BF16)
    return _dot(tri, hi) + _dot(tri, mid) + _dot(tri, lo)


def _gla_finish(rows, y, q, k, v, b, g_ref, out_ref, s_ref):
    n = b.shape[0]
    b_last = b[n - 1:n, :]
    kd = k * jnp.exp(b_last - b)
    upd = _dot_tn(kd.astype(BF16), v)
    decay_col = jnp.broadcast_to(jnp.exp(b_last), (LANES, GLA_DK)).T
    s_ref[...] = s_ref[...] * jnp.tile(decay_col, (1, GLA_DV // LANES)) + upd
    yn = y * lax.rsqrt(jnp.mean(y * y, axis=-1, keepdims=True) + EPS) * g_ref[...]
    out_ref[rows, :] = yn.astype(out_ref.dtype)


def _gla_chunk_direct(r0, n, b, q_ref, k_ref, v_ref, g_ref, out_ref, s_ref):
    rows = pl.ds(r0, n)
    q = q_ref[rows, :].astype(F32) * (GLA_DK ** -0.5)
    k = k_ref[rows, :].astype(F32)
    v = v_ref[rows, :]
    qe = (q * jnp.exp(b)).astype(BF16)
    cross = _dot(qe, s_ref[...].astype(BF16))
    scores = _dot_nt(qe, (k * jnp.exp(-b)).astype(BF16))
    causal = _iota((n, n), 0) >= _iota((n, n), 1)
    y = _dot(jnp.where(causal, scores, 0.0).astype(BF16), v) + cross
    _gla_finish(rows, y, q, k, v, b, g_ref, out_ref, s_ref)


def _gla_chunk_exact(r0, c_len, q_ref, k_ref, v_ref, z_ref, wgu_ref, bg_ref, g_ref, out_ref, s_ref):
    sub = GLA_SUB
    rows = pl.ds(r0, c_len)
    b = _gla_log_decay(rows, c_len, z_ref, wgu_ref, bg_ref)
    q = q_ref[rows, :].astype(F32) * (GLA_DK ** -0.5)
    k = k_ref[rows, :].astype(F32)
    v = v_ref[rows, :]
    cross = _dot((q * jnp.exp(b)).astype(BF16), s_ref[...].astype(BF16))

    key_row = _iota((c_len, 1), 0)
    lane = _iota((sub, c_len), 1)
    qrow = _iota((sub, c_len), 0)
    blocks = []
    for s in range(c_len // sub):
        sl = slice(sub * s, sub * (s + 1))
        bs, qs, ks = b[sl], q[sl], k[sl]
        acc = jnp.zeros((sub, c_len), F32)
        for j in range(sub):
            w = jnp.exp(jnp.minimum(bs - bs[j:j + 1, :], 0.0))
            col = jnp.sum(qs * w * ks[j:j + 1, :], axis=-1, keepdims=True)
            acc = jnp.where((lane == sub * s + j) & (qrow >= j), col, acc)
        if s > 0:
            b_ref_row = b[sub * s - 1:sub * s, :]
            qt = qs * jnp.exp(bs - b_ref_row)
            earlier = key_row < sub * s
            kt = jnp.where(earlier, k * jnp.exp(jnp.where(earlier, b_ref_row - b, 0.0)), 0.0)
            acc = acc + _dot_nt(qt.astype(BF16), kt.astype(BF16))
        blocks.append(acc)
    scores = jnp.concatenate(blocks, axis=0)
    y = _dot(scores.astype(BF16), v) + cross
    _gla_finish(rows, y, q, k, v, b, g_ref, out_ref, s_ref)


def _gla_chunk(r0, n, q_ref, k_ref, v_ref, z_ref, wgu_ref, bg_ref, g_ref, out_ref, s_ref):
    b = _gla_log_decay(pl.ds(r0, n), n, z_ref, wgu_ref, bg_ref)
    span = jnp.max(-b[n - 1:n, :])
    small = span <= GLA_SAFE_SPAN

    @pl.when(small)
    def _():
        _gla_chunk_direct(r0, n, b, q_ref, k_ref, v_ref, g_ref, out_ref, s_ref)

    @pl.when(jnp.logical_not(small))
    def _():
        c_len = min(n, GLA_EXACT_CHUNK)

        def body(i, carry):
            _gla_chunk_exact(pl.multiple_of(r0 + i * c_len, GLA_SUB), c_len,
                             q_ref, k_ref, v_ref, z_ref, wgu_ref, bg_ref, g_ref, out_ref, s_ref)
            return carry

        lax.fori_loop(0, n // c_len, body, 0)


def _gla_kernel(q_ref, k_ref, v_ref, z_ref, wgu_ref, bg_ref, g_ref, out_ref, s_ref):
    refs = (q_ref, k_ref, v_ref, z_ref, wgu_ref, bg_ref, g_ref, out_ref, s_ref)
    s_ref[...] = jnp.zeros_like(s_ref)
    _gla_chunk(0, N_META, *refs)

    def body(c, carry):
        _gla_chunk(pl.multiple_of(N_META + c * BIG_CHUNK, N_META), BIG_CHUNK, *refs)
        return carry

    lax.fori_loop(0, (q_ref.shape[0] - N_META) // BIG_CHUNK, body, 0)


def _gla(proj, z, wgu, bg, g_gla, *, batch, rows):
    m = proj.shape[0]
    return pl.pallas_call(
        _gla_kernel,
        out_shape=jax.ShapeDtypeStruct((m, D_MODEL), BF16),
        grid=(batch, GLA_HEADS),
        in_specs=[
            pl.BlockSpec((rows, GLA_DK), lambda b, h: (b, COL_Q_G // GLA_DK + h)),
            pl.BlockSpec((rows, GLA_DK), lambda b, h: (b, COL_K_G // GLA_DK + h)),
            pl.BlockSpec((rows, GLA_DV), lambda b, h: (b, COL_V_G // GLA_DV + h)),
            pl.BlockSpec((rows, GLA_RANK), lambda b, h: (b, 0)),
            pl.BlockSpec((GLA_RANK, GLA_DK), lambda b, h: (0, h)),
            pl.BlockSpec((1, GLA_DK), lambda b, h: (0, h)),
            pl.BlockSpec((1, GLA_DV), lambda b, h: (0, h)),
        ],
        out_specs=pl.BlockSpec((rows, GLA_DV), lambda b, h: (b, h)),
        scratch_shapes=[pltpu.VMEM((GLA_DK, GLA_DV), F32)],
        compiler_params=_params(("parallel", "parallel")),
        name="gla",
    )(proj, proj, proj, z, wgu, bg, g_gla.reshape(1, D_MODEL))


def _ret_chunk(r0, n, lg, q_ref, k_ref, v_ref, o_ref, m_ref, cos_ref, sin_ref, g_ref,
               yg_ref, og_ref, mg_ref, out_ref, s_ref):
    rows = pl.ds(r0, n)
    half = RET_DK // 2
    cos = cos_ref[rows, :]
    sin = sin_ref[rows, :]

    def rope(ref):
        x = ref[rows, :].astype(F32)
        x1, x2 = x[:, :half], x[:, half:]
        return jnp.concatenate([x1 * cos - x2 * sin, x2 * cos + x1 * sin], axis=-1)

    q = rope(q_ref)
    k = rope(k_ref) * (RET_DK ** -0.5)
    v = v_ref[rows, :]

    rel = (_iota((n, n), 0) - _iota((n, n), 1)).astype(F32)
    d_intra = jnp.where(rel >= 0, jnp.exp(lg * jnp.maximum(rel, 0.0)), 0.0)
    ridx = _iota((n, 1), 0).astype(F32)
    d_q = jnp.exp(lg * (ridx + 1.0))
    d_k = jnp.exp(lg * (n - 1.0 - ridx))
    d_chunk = jnp.exp(lg * float(n))

    qb = q.astype(BF16)
    scores = _dot_nt(qb, k.astype(BF16)) * d_intra
    state = s_ref[...]
    y = _dot(scores.astype(BF16), v) + _dot(qb, state.astype(BF16)) * d_q
    s_ref[...] = state * d_chunk + _dot_tn((k * d_k).astype(BF16), v)

    mu = jnp.mean(y, axis=-1, keepdims=True)
    yc = y - mu
    var = jnp.mean(yc * yc, axis=-1, keepdims=True)
    yn = yc * lax.rsqrt(var + EPS) * g_ref[...]

    def gated(branch, o_gate_ref, m_gate_ref):
        o = o_gate_ref[rows, :].astype(F32)
        return _sigmoid(m_gate_ref[rows, :].astype(F32)) * (branch * (o * _sigmoid(o)))

    merged = gated(yn, o_ref, m_ref) + gated(yg_ref[rows, :].astype(F32), og_ref, mg_ref)
    out_ref[rows, :] = merged.astype(out_ref.dtype)


def _ret_kernel(lg_ref, q_ref, k_ref, v_ref, o_ref, m_ref, cos_ref, sin_ref, g_ref,
                yg_ref, og_ref, mg_ref, out_ref, s_ref):
    lg = lg_ref[0][:, :1]
    refs = (q_ref, k_ref, v_ref, o_ref, m_ref, cos_ref, sin_ref, g_ref, yg_ref, og_ref, mg_ref,
            out_ref, s_ref)
    s_ref[...] = jnp.zeros_like(s_ref)
    _ret_chunk(0, N_META, lg, *refs)

    def body(c, carry):
        r0 = pl.multiple_of(N_META + c * BIG_CHUNK, N_META)
        _ret_chunk(r0, BIG_CHUNK, lg, *refs)
        return carry

    lax.fori_loop(0, (q_ref.shape[0] - N_META) // BIG_CHUNK, body, 0)


def _retention_merge(proj, y_gla, cos, sin, g_ret, *, batch, rows):
    m = proj.shape[0]
    log_gamma = jnp.log1p(-jnp.exp2(-5.0 - jnp.arange(RET_HEADS, dtype=F32)))
    lg = jnp.broadcast_to(log_gamma[:, None, None], (RET_HEADS, 1, LANES))

    def cols(start, width):
        return lambda b, h: (b, start // width + h)

    dk, dv = RET_DK, RET_DV
    return pl.pallas_call(
        _ret_kernel,
        out_shape=jax.ShapeDtypeStruct((m, D_MODEL), BF16),
        grid=(batch, RET_HEADS),
        in_specs=[
            pl.BlockSpec((1, 1, LANES), lambda b, h: (h, 0, 0)),
            pl.BlockSpec((rows, dk), cols(COL_Q_R, dk)),
            pl.BlockSpec((rows, dk), cols(COL_K_R, dk)),
            pl.BlockSpec((rows, dv), cols(COL_V_R, dv)),
            pl.BlockSpec((rows, dv), cols(COL_O_R, dv)),
            pl.BlockSpec((rows, dv), cols(COL_M_R, dv)),
            pl.BlockSpec((rows, dk // 2), lambda b, h: (b, 0)),
            pl.BlockSpec((rows, dk // 2), lambda b, h: (b, 0)),
            pl.BlockSpec((1, dv), lambda b, h: (0, h)),
            pl.BlockSpec((rows, dv), lambda b, h: (b, h)),
            pl.BlockSpec((rows, dv), cols(COL_O_G, dv)),
            pl.BlockSpec((rows, dv), cols(COL_M_G, dv)),
        ],
        out_specs=pl.BlockSpec((rows, dv), lambda b, h: (b, h)),
        scratch_shapes=[pltpu.VMEM((dk, dv), F32)],
        compiler_params=_params(("parallel", "parallel")),
        name="retention_merge",
    )(lg, proj, proj, proj, proj, proj, cos, sin, g_ret.reshape(1, D_MODEL), y_gla, proj, proj)


def _ffn_in_kernel(a_ref, wu_ref, wg_ref, cw_ref, cb_ref, hid_ref, up_ref):
    rows = a_ref.shape[0]
    tf = wu_ref.shape[1]
    w = jnp.concatenate([wu_ref[...].astype(BF16), wg_ref[...].astype(BF16)], axis=1)
    cw = cw_ref[...]
    cb = cb_ref[...]
    up_ref[0:8, :] = jnp.zeros((8, tf), F32)
    starts = list(range(0, rows - FFN_ROW_CHUNK + 1, FFN_ROW_CHUNK))
    for lo, hi in zip(starts, starts[1:] + [rows]):
        up_gate = _dot(a_ref[lo:hi, :], w)
        up, gate = up_gate[:, :tf], up_gate[:, tf:]
        up_ref[8 + lo:8 + hi, :] = up
        c = (cb + cw[0:1, :] * up_ref[6 + lo:6 + hi, :] + cw[1:2, :] * up_ref[7 + lo:7 + hi, :]
             + cw[2:3, :] * up)
        hid_ref[lo:hi, :] = (c * _sigmoid(c) * gate).astype(hid_ref.dtype)


def _ffn_in(a, w_ffn_in, conv_w, conv_b, *, rows, tf):
    m, k = a.shape
    nf = D_FF // tf
    return pl.pallas_call(
        _ffn_in_kernel,
        out_shape=jax.ShapeDtypeStruct((m, D_FF), BF16),
        grid=(m // rows, nf),
        in_specs=[
            pl.BlockSpec((rows, k), lambda i, j: (i, 0), pipeline_mode=pl.Buffered(1)),
            pl.BlockSpec((k, tf), lambda i, j: (0, j)),
            pl.BlockSpec((k, tf), lambda i, j: (0, nf + j)),
            pl.BlockSpec((CONV_W, tf), lambda i, j: (0, j)),
            pl.BlockSpec((1, tf), lambda i, j: (0, j)),
        ],
        out_specs=pl.BlockSpec((rows, tf), lambda i, j: (i, j)),
        scratch_shapes=[pltpu.VMEM((rows + 8, tf), F32)],
        compiler_params=_params(("parallel", "parallel")),
        name="ffn_in_conv",
    )(a, w_ffn_in, w_ffn_in, conv_w, conv_b.reshape(1, D_FF))


def kernel(x, positions, meta_tokens, attn_norm, w_in, w_gate_up, b_gate, ret_norm, gla_norm,
           w_out, ffn_norm, w_ffn_in, conv_w, conv_b, w_ffn_out, final_norm):
    batch, seq, d = x.shape
    rows = N_META + seq
    m = batch * rows

    meta = jnp.broadcast_to(meta_tokens[None].astype(x.dtype), (batch, N_META, d))
    h0 = jnp.concatenate([meta, x], axis=1).reshape(m, d)
    pos = jnp.concatenate([
        jnp.broadcast_to(jnp.arange(N_META, dtype=jnp.int32), (batch, N_META)),
        positions.astype(jnp.int32) + N_META], axis=1).reshape(m, 1)
    half = RET_DK // 2
    inv_freq = (ROPE_BASE ** (-jnp.arange(half, dtype=F32) / half)).reshape(1, half)

    w_in_t = jnp.swapaxes(w_in, 1, 2)[0]
    w_ffn_out_b = w_ffn_out[0].astype(BF16)

    cos, sin = _rope_tables(pos, inv_freq, rows=rows)

    norm_rows = rows // 3
    hn = _rmsnorm(h0, attn_norm[0], rows=norm_rows, out_dtype=BF16)
    proj, z = _in_proj(hn, w_in_t, tm=rows, tn=512)

    y_gla = _gla(proj, z, w_gate_up[0].astype(BF16), b_gate[0].reshape(1, -1), gla_norm[0],
                 batch=batch, rows=rows)
    merged = _retention_merge(proj, y_gla, cos, sin, ret_norm[0], batch=batch, rows=rows)

    h1 = _matmul_res(merged, w_out[0], h0, tm=rows, tn=256, a_buffers=1, name="out_proj")
    h1n = _rmsnorm(h1, ffn_norm[0], rows=norm_rows, out_dtype=BF16)
    hidden = _ffn_in(h1n, w_ffn_in[0], conv_w[0], conv_b[0], rows=rows, tf=256)
    h2 = _matmul_res(hidden, w_ffn_out_b, h1, tm=norm_rows, tn=256, a_buffers=2, name="ffn_out")
    return _final_norm(h2, final_norm, batch=batch, rows_per_batch=rows, seq=seq, rows=256)
```

```python
import jax
import jax.numpy as jnp
from jax import lax
from jax.experimental import pallas as pl
from jax.experimental.pallas import tpu as pltpu

F32 = jnp.float32
BF16 = jnp.bfloat16

D_MODEL = 4096
N_META = 16
RET_HEADS, RET_DK, RET_DV = 8, 256, 512
GLA_HEADS, GLA_DK, GLA_DV = 4, 512, 1024
GLA_RANK = 16
GLA_GATE_TAU = 16.0
D_FF = 11008
CONV_W = 3
ROPE_BASE = 10000.0
EPS = 1e-6
RET_QK = RET_HEADS * RET_DK
GLA_QK = GLA_HEADS * GLA_DK
COL_Q_R, COL_K_R = 0, RET_QK
COL_V_R, COL_O_R = 2 * RET_QK, 2 * RET_QK + D_MODEL
COL_Q_G = 2 * RET_QK + 2 * D_MODEL
COL_K_G, COL_V_G = COL_Q_G + GLA_QK, COL_Q_G + 2 * GLA_QK
COL_O_G = COL_V_G + D_MODEL
N_MAIN = COL_O_G + D_MODEL
COL_M_R, COL_M_G = N_MAIN, N_MAIN + D_MODEL
N_PROJ = N_MAIN + 2 * D_MODEL

VMEM_LIMIT_V7X = 56 * 1024 * 1024
LANES = 128
BIG_CHUNK = 256
FFN_ROW_CHUNK = 256
GLA_EXACT_CHUNK = 64
GLA_SUB = 16
GLA_SAFE_SPAN = 60.0


def _params(sem):
    return pltpu.CompilerParams(dimension_semantics=sem, vmem_limit_bytes=VMEM_LIMIT_V7X)


def _sigmoid(x):
    return 0.5 * jnp.tanh(0.5 * x) + 0.5


def _iota(shape, dim):
    return lax.broadcasted_iota(jnp.int32, shape, dim)


def _dot(a, b):
    return jnp.dot(a, b, preferred_element_type=F32)


def _dot_nt(a, b):
    return lax.dot_general(a, b, (((1,), (1,)), ((), ())), preferred_element_type=F32)


def _dot_tn(a, b):
    return lax.dot_general(a, b, (((0,), (0,)), ((), ())), preferred_element_type=F32)


def _rmsnorm_kernel(x_ref, g_ref, o_ref):
    x = x_ref[...]
    y = x * lax.rsqrt(jnp.mean(x * x, axis=-1, keepdims=True) + EPS)
    o_ref[...] = (y * g_ref[...]).astype(o_ref.dtype)


def _rmsnorm(x, g, *, rows, out_dtype):
    m, d = x.shape
    return pl.pallas_call(
        _rmsnorm_kernel,
        out_shape=jax.ShapeDtypeStruct((m, d), out_dtype),
        grid=(m // rows,),
        in_specs=[pl.BlockSpec((rows, d), lambda i: (i, 0)),
                  pl.BlockSpec((1, d), lambda i: (0, 0))],
        out_specs=pl.BlockSpec((rows, d), lambda i: (i, 0)),
        compiler_params=_params(("parallel",)),
        name="rmsnorm",
    )(x, g.reshape(1, d))


def _final_norm_kernel(x_ref, g_ref, o_ref):
    x = x_ref[...]
    y = x * lax.rsqrt(jnp.mean(x * x, axis=-1, keepdims=True) + EPS)
    o_ref[...] = y * g_ref[...]


def _final_norm(h, g, *, batch, rows_per_batch, seq, rows):
    d = h.shape[1]
    lead = rows_per_batch - seq
    return pl.pallas_call(
        _final_norm_kernel,
        out_shape=jax.ShapeDtypeStruct((batch, seq, d), F32),
        grid=(batch, seq // rows),
        in_specs=[pl.BlockSpec((pl.Element(1), pl.Element(rows), pl.Element(d)),
                               lambda b, j: (b, pl.multiple_of(lead + rows * j, N_META), 0)),
                  pl.BlockSpec((1, d), lambda b, j: (0, 0))],
        out_specs=pl.BlockSpec((1, rows, d), lambda b, j: (b, j, 0)),
        compiler_params=_params(("parallel", "parallel")),
        name="final_norm",
    )(h.reshape(batch, rows_per_batch, d), g.reshape(1, d))


def _mm_res_kernel(a_ref, w_ref, r_ref, o_ref):
    o_ref[...] = (r_ref[...] + _dot(a_ref[...], w_ref[...].astype(BF16))).astype(o_ref.dtype)


def _matmul_res(a, w, residual, *, tm, tn, a_buffers, name):
    m, k = a.shape
    n = w.shape[1]
    return pl.pallas_call(
        _mm_res_kernel,
        out_shape=jax.ShapeDtypeStruct((m, n), F32),
        grid=(m // tm, n // tn),
        in_specs=[pl.BlockSpec((tm, k), lambda i, j: (i, 0), pipeline_mode=pl.Buffered(a_buffers)),
                  pl.BlockSpec((k, tn), lambda i, j: (0, j)),
                  pl.BlockSpec((tm, tn), lambda i, j: (i, j))],
        out_specs=pl.BlockSpec((tm, tn), lambda i, j: (i, j)),
        compiler_params=_params(("parallel", "parallel")),
        name=name,
    )(a, w, residual)


def _in_proj_kernel(a_ref, wt_ref, wz_ref, o_ref, z_ref):
    o_ref[...] = _dot_nt(a_ref[...], wt_ref[...].astype(BF16)).astype(o_ref.dtype)

    @pl.when(pl.program_id(1) == 0)
    def _():
        z_ref[...] = _dot_nt(a_ref[...], wz_ref[...].astype(BF16))


def _in_proj(a, w_in_t, *, tm, tn):
    m, k = a.shape

    def row_start(j):
        return pl.multiple_of(j * tn + jnp.where(j * tn >= N_MAIN, GLA_RANK, 0), GLA_RANK)

    return pl.pallas_call(
        _in_proj_kernel,
        out_shape=(jax.ShapeDtypeStruct((m, N_PROJ), BF16), jax.ShapeDtypeStruct((m, GLA_RANK), F32)),
        grid=(m // tm, N_PROJ // tn),
        in_specs=[pl.BlockSpec((tm, k), lambda i, j: (i, 0), pipeline_mode=pl.Buffered(1)),
                  pl.BlockSpec((pl.Element(tn), pl.Element(k)), lambda i, j: (row_start(j), 0)),
                  pl.BlockSpec((GLA_RANK, k), lambda i, j: (N_MAIN // GLA_RANK, 0))],
        out_specs=(pl.BlockSpec((tm, tn), lambda i, j: (i, j)),
                   pl.BlockSpec((tm, GLA_RANK), lambda i, j: (i, 0))),
        compiler_params=_params(("parallel", "arbitrary")),
        name="in_proj",
    )(a, w_in_t, w_in_t)


def _rope_table_kernel(pos_ref, invf_ref, cos_ref, sin_ref):
    ang = pos_ref[...].astype(F32) * invf_ref[...]
    cos_ref[...] = jnp.cos(ang)
    sin_ref[...] = jnp.sin(ang)


def _rope_tables(pos_col, inv_freq, *, rows):
    m = pos_col.shape[0]
    half = inv_freq.shape[1]
    out = jax.ShapeDtypeStruct((m, half), F32)
    return pl.pallas_call(
        _rope_table_kernel,
        out_shape=(out, out),
        grid=(m // rows,),
        in_specs=[pl.BlockSpec((rows, 1), lambda i: (i, 0)),
                  pl.BlockSpec((1, half), lambda i: (0, 0))],
        out_specs=(pl.BlockSpec((rows, half), lambda i: (i, 0)),
                   pl.BlockSpec((rows, half), lambda i: (i, 0))),
        compiler_params=_params(("parallel",)),
        name="rope_tables",
    )(pos_col, inv_freq)


def _lower_tri(n):
    return (_iota((n, n), 0) >= _iota((n, n), 1)).astype(BF16)


def _gla_log_decay(rows, tri, z_ref, wgu_ref, bg_ref):
    u = _dot(z_ref[rows, :].astype(BF16), wgu_ref[...]) + bg_ref[...]
    log_a = (jnp.minimum(u, 0.0) - jnp.log(1.0 + jnp.exp(-jnp.abs(u)))) * (1.0 / GLA_GATE_TAU)
    hi = log_a.astype(BF16)
    r1 = log_a - hi.astype(F32)
    mid = r1.astype(BF16)
    lo = (r1 - mid.astype(F32)).astype(BF16)
    return _dot(tri, hi) + _dot(tri, mid) + _dot(tri, lo)


def _gla_finish(rows, y, q, k, v, b, g_ref, out_ref, s_ref):
    n = b.shape[0]
    b_last = b[n - 1:n, :]
    kd = k * jnp.exp(b_last - b)
    upd = _dot_tn(kd.astype(BF16), v)
    decay_col = jnp.broadcast_to(jnp.exp(b_last), (LANES, GLA_DK)).T
    s_ref[...] = s_ref[...] * jnp.tile(decay_col, (1, GLA_DV // LANES)) + upd
    yn = y * lax.rsqrt(jnp.mean(y * y, axis=-1, keepdims=True) + EPS) * g_ref[...]
    out_ref[rows, :] = yn.astype(out_ref.dtype)


def _gla_chunk_direct(r0, n, b, q_ref, k_ref, v_ref, g_ref, out_ref, s_ref):
    rows = pl.ds(r0, n)
    q = q_ref[rows, :].astype(F32) * (GLA_DK ** -0.5)
    k = k_ref[rows, :].astype(F32)
    v = v_ref[rows, :]
    qe = (q * jnp.exp(b)).astype(BF16)
    cross = _dot(qe, s_ref[...].astype(BF16))
    scores = _dot_nt(qe, (k * jnp.exp(-b)).astype(BF16))
    causal = _iota((n, n), 0) >= _iota((n, n), 1)
    y = _dot(jnp.where(causal, scores, 0.0).astype(BF16), v) + cross
    _gla_finish(rows, y, q, k, v, b, g_ref, out_ref, s_ref)


def _gla_chunk_exact(r0, c_len, q_ref, k_ref, v_ref, z_ref, wgu_ref, bg_ref, g_ref, out_ref, s_ref):
    sub = GLA_SUB
    rows = pl.ds(r0, c_len)
    b = _gla_log_decay(rows, _lower_tri(c_len), z_ref, wgu_ref, bg_ref)
    q = q_ref[rows, :].astype(F32) * (GLA_DK ** -0.5)
    k = k_ref[rows, :].astype(F32)
    v = v_ref[rows, :]
    cross = _dot((q * jnp.exp(b)).astype(BF16), s_ref[...].astype(BF16))

    key_row = _iota((c_len, 1), 0)
    lane = _iota((sub, c_len), 1)
    qrow = _iota((sub, c_len), 0)
    blocks = []
    for s in range(c_len // sub):
        sl = slice(sub * s, sub * (s + 1))
        bs, qs, ks = b[sl], q[sl], k[sl]
        acc = jnp.zeros((sub, c_len), F32)
        for j in range(sub):
            w = jnp.exp(jnp.minimum(bs - bs[j:j + 1, :], 0.0))
            col = jnp.sum(qs * w * ks[j:j + 1, :], axis=-1, keepdims=True)
            acc = jnp.where((lane == sub * s + j) & (qrow >= j), col, acc)
        if s > 0:
            b_ref_row = b[sub * s - 1:sub * s, :]
            qt = qs * jnp.exp(bs - b_ref_row)
            earlier = key_row < sub * s
            kt = jnp.where(earlier, k * jnp.exp(jnp.where(earlier, b_ref_row - b, 0.0)), 0.0)
            acc = acc + _dot_nt(qt.astype(BF16), kt.astype(BF16))
        blocks.append(acc)
    scores = jnp.concatenate(blocks, axis=0)
    y = _dot(scores.astype(BF16), v) + cross
    _gla_finish(rows, y, q, k, v, b, g_ref, out_ref, s_ref)


def _gla_chunk(r0, tri, q_ref, k_ref, v_ref, z_ref, wgu_ref, bg_ref, g_ref, out_ref, s_ref):
    n = tri.shape[0]
    b = _gla_log_decay(pl.ds(r0, n), tri, z_ref, wgu_ref, bg_ref)
    span = jnp.max(-b[n - 1:n, :])
    small = span <= GLA_SAFE_SPAN

    @pl.when(small)
    def _():
        _gla_chunk_direct(r0, n, b, q_ref, k_ref, v_ref, g_ref, out_ref, s_ref)

    @pl.when(jnp.logical_not(small))
    def _():
        c_len = min(n, GLA_EXACT_CHUNK)

        def body(i, carry):
            _gla_chunk_exact(pl.multiple_of(r0 + i * c_len, GLA_SUB), c_len,
                             q_ref, k_ref, v_ref, z_ref, wgu_ref, bg_ref, g_ref, out_ref, s_ref)
            return carry

        lax.fori_loop(0, n // c_len, body, 0)


def _gla_kernel(q_ref, k_ref, v_ref, z_ref, wgu_ref, bg_ref, g_ref, out_ref, s_ref):
    refs = (q_ref, k_ref, v_ref, z_ref, wgu_ref, bg_ref, g_ref, out_ref, s_ref)
    s_ref[...] = jnp.zeros_like(s_ref)
    _gla_chunk(0, _lower_tri(N_META), *refs)
    tri = _lower_tri(BIG_CHUNK)

    def body(c, carry):
        _gla_chunk(pl.multiple_of(N_META + c * BIG_CHUNK, N_META), tri, *refs)
        return carry

    lax.fori_loop(0, (q_ref.shape[0] - N_META) // BIG_CHUNK, body, 0)


def _gla(proj, z, wgu, bg, g_gla, *, batch, rows):
    m = proj.shape[0]
    return pl.pallas_call(
        _gla_kernel,
        out_shape=jax.ShapeDtypeStruct((m, D_MODEL), BF16),
        grid=(batch, GLA_HEADS),
        in_specs=[
            pl.BlockSpec((rows, GLA_DK), lambda b, h: (b, COL_Q_G // GLA_DK + h)),
            pl.BlockSpec((rows, GLA_DK), lambda b, h: (b, COL_K_G // GLA_DK + h)),
            pl.BlockSpec((rows, GLA_DV), lambda b, h: (b, COL_V_G // GLA_DV + h)),
            pl.BlockSpec((rows, GLA_RANK), lambda b, h: (b, 0)),
            pl.BlockSpec((GLA_RANK, GLA_DK), lambda b, h: (0, h)),
            pl.BlockSpec((1, GLA_DK), lambda b, h: (0, h)),
            pl.BlockSpec((1, GLA_DV), lambda b, h: (0, h)),
        ],
        out_specs=pl.BlockSpec((rows, GLA_DV), lambda b, h: (b, h)),
        scratch_shapes=[pltpu.VMEM((GLA_DK, GLA_DV), F32)],
        compiler_params=_params(("parallel", "parallel")),
        name="gla",
    )(proj, proj, proj, z, wgu, bg, g_gla.reshape(1, D_MODEL))


def _ret_decays(lg, n):
    rel = (_iota((n, n), 0) - _iota((n, n), 1)).astype(F32)
    d_intra = jnp.where(rel >= 0, jnp.exp(lg * jnp.maximum(rel, 0.0)), 0.0)
    ridx = _iota((n, 1), 0).astype(F32)
    d_q = jnp.exp(lg * (ridx + 1.0))
    d_k = jnp.exp(lg * (n - 1.0 - ridx))
    d_chunk = jnp.exp(lg * float(n))
    return d_intra, d_q, d_k, d_chunk


def _ret_chunk(r0, decays, q_ref, k_ref, v_ref, o_ref, m_ref, cos_ref, sin_ref, g_ref,
               yg_ref, og_ref, mg_ref, out_ref, s_ref):
    d_intra, d_q, d_k, d_chunk = decays
    n = d_intra.shape[0]
    rows = pl.ds(r0, n)
    half = RET_DK // 2
    cos = cos_ref[rows, :]
    sin = sin_ref[rows, :]

    def rope(ref):
        x = ref[rows, :].astype(F32)
        x1, x2 = x[:, :half], x[:, half:]
        return jnp.concatenate([x1 * cos - x2 * sin, x2 * cos + x1 * sin], axis=-1)

    q = rope(q_ref)
    k = rope(k_ref) * (RET_DK ** -0.5)
    v = v_ref[rows, :]

    qb = q.astype(BF16)
    scores = _dot_nt(qb, k.astype(BF16)) * d_intra
    state = s_ref[...]
    y = _dot(scores.astype(BF16), v) + _dot(qb, state.astype(BF16)) * d_q
    s_ref[...] = state * d_chunk + _dot_tn((k * d_k).astype(BF16), v)

    mu = jnp.mean(y, axis=-1, keepdims=True)
    yc = y - mu
    var = jnp.mean(yc * yc, axis=-1, keepdims=True)
    yn = yc * lax.rsqrt(var + EPS) * g_ref[...]

    def gated(branch, o_gate_ref, m_gate_ref):
        o = o_gate_ref[rows, :].astype(F32)
        return _sigmoid(m_gate_ref[rows, :].astype(F32)) * (branch * (o * _sigmoid(o)))

    merged = gated(yn, o_ref, m_ref) + gated(yg_ref[rows, :].astype(F32), og_ref, mg_ref)
    out_ref[rows, :] = merged.astype(out_ref.dtype)


def _ret_kernel(lg_ref, q_ref, k_ref, v_ref, o_ref, m_ref, cos_ref, sin_ref, g_ref,
                yg_ref, og_ref, mg_ref, out_ref, s_ref):
    lg = lg_ref[0][:, :1]
    refs = (q_ref, k_ref, v_ref, o_ref, m_ref, cos_ref, sin_ref, g_ref, yg_ref, og_ref, mg_ref,
            out_ref, s_ref)
    s_ref[...] = jnp.zeros_like(s_ref)
    _ret_chunk(0, _ret_decays(lg, N_META), *refs)
    decays = _ret_decays(lg, BIG_CHUNK)

    def body(c, carry):
        r0 = pl.multiple_of(N_META + c * BIG_CHUNK, N_META)
        _ret_chunk(r0, decays, *refs)
        return carry

    lax.fori_loop(0, (q_ref.shape[0] - N_META) // BIG_CHUNK, body, 0)


def _retention_merge(proj, y_gla, cos, sin, g_ret, *, batch, rows):
    m = proj.shape[0]
    log_gamma = jnp.log1p(-jnp.exp2(-5.0 - jnp.arange(RET_HEADS, dtype=F32)))
    lg = jnp.broadcast_to(log_gamma[:, None, None], (RET_HEADS, 1, LANES))

    def cols(start, width):
        return lambda b, h: (b, start // width + h)

    dk, dv = RET_DK, RET_DV
    return pl.pallas_call(
        _ret_kernel,
        out_shape=jax.ShapeDtypeStruct((m, D_MODEL), BF16),
        grid=(batch, RET_HEADS),
        in_specs=[
            pl.BlockSpec((1, 1, LANES), lambda b, h: (h, 0, 0)),
            pl.BlockSpec((rows, dk), cols(COL_Q_R, dk)),
            pl.BlockSpec((rows, dk), cols(COL_K_R, dk)),
            pl.BlockSpec((rows, dv), cols(COL_V_R, dv)),
            pl.BlockSpec((rows, dv), cols(COL_O_R, dv)),
            pl.BlockSpec((rows, dv), cols(COL_M_R, dv)),
            pl.BlockSpec((rows, dk // 2), lambda b, h: (b, 0)),
            pl.BlockSpec((rows, dk // 2), lambda b, h: (b, 0)),
            pl.BlockSpec((1, dv), lambda b, h: (0, h)),
            pl.BlockSpec((rows, dv), lambda b, h: (b, h)),
            pl.BlockSpec((rows, dv), cols(COL_O_G, dv)),
            pl.BlockSpec((rows, dv), cols(COL_M_G, dv)),
        ],
        out_specs=pl.BlockSpec((rows, dv), lambda b, h: (b, h)),
        scratch_shapes=[pltpu.VMEM((dk, dv), F32)],
        compiler_params=_params(("parallel", "parallel")),
        name="retention_merge",
    )(lg, proj, proj, proj, proj, proj, cos, sin, g_ret.reshape(1, D_MODEL), y_gla, proj, proj)


def _ffn_in_kernel(a_ref, wu_ref, wg_ref, cw_ref, cb_ref, wo_ref, hid_ref, wo_bf16_ref, up_ref):
    wo_bf16_ref[...] = wo_ref[...].astype(BF16)
    rows = a_ref.shape[0]
    tf = wu_ref.shape[1]
    w = jnp.concatenate([wu_ref[...].astype(BF16), wg_ref[...].astype(BF16)], axis=1)
    cw = cw_ref[...]
    cb = cb_ref[...]
    up_ref[0:8, :] = jnp.zeros((8, tf), F32)
    starts = list(range(0, rows - FFN_ROW_CHUNK + 1, FFN_ROW_CHUNK))
    for lo, hi in zip(starts, starts[1:] + [rows]):
        up_gate = _dot(a_ref[lo:hi, :], w)
        up, gate = up_gate[:, :tf], up_gate[:, tf:]
        up_ref[8 + lo:8 + hi, :] = up
        c = (cb + cw[0:1, :] * up_ref[6 + lo:6 + hi, :] + cw[1:2, :] * up_ref[7 + lo:7 + hi, :]
             + cw[2:3, :] * up)
        hid_ref[lo:hi, :] = (c * _sigmoid(c) * gate).astype(hid_ref.dtype)


def _ffn_in(a, w_ffn_in, conv_w, conv_b, w_ffn_out, *, rows, tf):
    m, k = a.shape
    nf = D_FF // tf
    n_steps = (m // rows) * nf
    d_out = w_ffn_out.shape[1]
    slab = D_FF // n_steps
    assert slab * n_steps == D_FF and slab % 16 == 0
    return pl.pallas_call(
        _ffn_in_kernel,
        out_shape=(jax.ShapeDtypeStruct((m, D_FF), BF16),
                   jax.ShapeDtypeStruct((D_FF, d_out), BF16)),
        grid=(m // rows, nf),
        in_specs=[
            pl.BlockSpec((rows, k), lambda i, j: (i, 0), pipeline_mode=pl.Buffered(1)),
            pl.BlockSpec((k, tf), lambda i, j: (0, j)),
            pl.BlockSpec((k, tf), lambda i, j: (0, nf + j)),
            pl.BlockSpec((CONV_W, tf), lambda i, j: (0, j)),
            pl.BlockSpec((1, tf), lambda i, j: (0, j)),
            pl.BlockSpec((slab, d_out), lambda i, j: (i * nf + j, 0)),
        ],
        out_specs=(pl.BlockSpec((rows, tf), lambda i, j: (i, j)),
                   pl.BlockSpec((slab, d_out), lambda i, j: (i * nf + j, 0))),
        scratch_shapes=[pltpu.VMEM((rows + 8, tf), F32)],
        compiler_params=_params(("parallel", "parallel")),
        name="ffn_in_conv",
    )(a, w_ffn_in, w_ffn_in, conv_w, conv_b.reshape(1, D_FF), w_ffn_out)


def kernel(x, positions, meta_tokens, attn_norm, w_in, w_gate_up, b_gate, ret_norm, gla_norm,
           w_out, ffn_norm, w_ffn_in, conv_w, conv_b, w_ffn_out, final_norm):
    batch, seq, d = x.shape
    rows = N_META + seq
    m = batch * rows

    meta = jnp.broadcast_to(meta_tokens[None].astype(x.dtype), (batch, N_META, d))
    h0 = jnp.concatenate([meta, x], axis=1).reshape(m, d)
    pos = jnp.concatenate([
        jnp.broadcast_to(jnp.arange(N_META, dtype=jnp.int32), (batch, N_META)),
        positions.astype(jnp.int32) + N_META], axis=1).reshape(m, 1)
    half = RET_DK // 2
    inv_freq = (ROPE_BASE ** (-jnp.arange(half, dtype=F32) / half)).reshape(1, half)

    w_in_t = jnp.swapaxes(w_in, 1, 2)[0]

    cos, sin = _rope_tables(pos, inv_freq, rows=rows)

    norm_rows = rows // 3
    hn = _rmsnorm(h0, attn_norm[0], rows=norm_rows, out_dtype=BF16)
    proj, z = _in_proj(hn, w_in_t, tm=rows, tn=512)

    y_gla = _gla(proj, z, w_gate_up[0].astype(BF16), b_gate[0].reshape(1, -1), gla_norm[0],
                 batch=batch, rows=rows)
    merged = _retention_merge(proj, y_gla, cos, sin, ret_norm[0], batch=batch, rows=rows)

    h1 = _matmul_res(merged, w_out[0], h0, tm=rows, tn=256, a_buffers=1, name="out_proj")
    h1n = _rmsnorm(h1, ffn_norm[0], rows=norm_rows, out_dtype=BF16)
    hidden, w_ffn_out_b = _ffn_in(h1n, w_ffn_in[0], conv_w[0], conv_b[0], w_ffn_out[0],
                                  rows=rows, tf=256)
    h2 = _matmul_res(hidden, w_ffn_out_b, h1, tm=norm_rows, tn=256, a_buffers=2, name="ffn_out")
    return _final_norm(h2, final_norm, batch=batch, rows_per_batch=rows, seq=seq, rows=256)
```

```python
import jax
import jax.numpy as jnp
from jax import lax
from jax.experimental import pallas as pl
from jax.experimental.pallas import tpu as pltpu

F32 = jnp.float32
BF16 = jnp.bfloat16

D_MODEL = 4096
N_META = 16
RET_HEADS, RET_DK, RET_DV = 8, 256, 512
GLA_HEADS, GLA_DK, GLA_DV = 4, 512, 1024
GLA_RANK = 16
GLA_GATE_TAU = 16.0
D_FF = 11008
CONV_W = 3
ROPE_BASE = 10000.0
EPS = 1e-6
RET_QK = RET_HEADS * RET_DK
GLA_QK = GLA_HEADS * GLA_DK
COL_Q_R, COL_K_R = 0, RET_QK
COL_V_R, COL_O_R = 2 * RET_QK, 2 * RET_QK + D_MODEL
COL_Q_G = 2 * RET_QK + 2 * D_MODEL
COL_K_G, COL_V_G = COL_Q_G + GLA_QK, COL_Q_G + 2 * GLA_QK
COL_O_G = COL_V_G + D_MODEL
N_MAIN = COL_O_G + D_MODEL
COL_M_R, COL_M_G = N_MAIN, N_MAIN + D_MODEL
N_PROJ = N_MAIN + 2 * D_MODEL

VMEM_LIMIT_V7X = 56 * 1024 * 1024
LANES = 128
BIG_CHUNK = 256
FFN_ROW_CHUNK = 256
GLA_EXACT_CHUNK = 64
GLA_SUB = 16
GLA_SAFE_SPAN = 60.0


def _params(sem):
    return pltpu.CompilerParams(dimension_semantics=sem, vmem_limit_bytes=VMEM_LIMIT_V7X)


def _iota(shape, dim):
    return lax.broadcasted_iota(jnp.int32, shape, dim)


def _dot(a, b):
    return jnp.dot(a, b, preferred_element_type=F32)


def _dot_nt(a, b):
    return lax.dot_general(a, b, (((1,), (1,)), ((), ())), preferred_element_type=F32)


def _dot_tn(a, b):
    return lax.dot_general(a, b, (((0,), (0,)), ((), ())), preferred_element_type=F32)


def _rmsnorm_kernel(x_ref, g_ref, o_ref):
    x = x_ref[...]
    y = x * lax.rsqrt(jnp.mean(x * x, axis=-1, keepdims=True) + EPS)
    o_ref[...] = (y * g_ref[...]).astype(o_ref.dtype)


def _rmsnorm(x, g, *, rows, out_dtype):
    m, d = x.shape
    return pl.pallas_call(
        _rmsnorm_kernel,
        out_shape=jax.ShapeDtypeStruct((m, d), out_dtype),
        grid=(m // rows,),
        in_specs=[pl.BlockSpec((rows, d), lambda i: (i, 0)),
                  pl.BlockSpec((1, d), lambda i: (0, 0))],
        out_specs=pl.BlockSpec((rows, d), lambda i: (i, 0)),
        compiler_params=_params(("parallel",)),
        name="rmsnorm",
    )(x, g.reshape(1, d))


def _embed_norm_kernel(x_ref, meta_ref, g_ref, o_ref):
    def norm(v):
        y = v * lax.rsqrt(jnp.mean(v * v, axis=-1, keepdims=True) + EPS)
        return (y * g_ref[...]).astype(o_ref.dtype)

    r = pl.program_id(1)
    rows = o_ref.shape[0]

    @pl.when(r == 0)
    def _():
        o_ref[0:N_META, :] = norm(meta_ref[...])
        o_ref[N_META:, :] = norm(x_ref[0, 0:rows - N_META, :])

    @pl.when(r > 0)
    def _():
        o_ref[...] = norm(x_ref[0])


def _embed_norm(x, meta_tokens, g, *, rows, out_dtype):
    batch, seq, d = x.shape
    tiles = (N_META + seq) // rows

    def x_start(r):
        return pl.multiple_of(jnp.maximum(rows * r - N_META, 0), N_META)

    return pl.pallas_call(
        _embed_norm_kernel,
        out_shape=jax.ShapeDtypeStruct((batch * (N_META + seq), d), out_dtype),
        grid=(batch, tiles),
        in_specs=[pl.BlockSpec((pl.Element(1), pl.Element(rows), pl.Element(d)),
                               lambda b, r: (b, x_start(r), 0)),
                  pl.BlockSpec((N_META, d), lambda b, r: (0, 0)),
                  pl.BlockSpec((1, d), lambda b, r: (0, 0))],
        out_specs=pl.BlockSpec((rows, d), lambda b, r: (b * tiles + r, 0)),
        compiler_params=_params(("parallel", "parallel")),
        name="embed_norm",
    )(x, meta_tokens, g.reshape(1, d))


def _final_norm_kernel(x_ref, g_ref, o_ref):
    x = x_ref[...]
    y = x * lax.rsqrt(jnp.mean(x * x, axis=-1, keepdims=True) + EPS)
    o_ref[...] = y * g_ref[...]


def _final_norm(h, g, *, batch, rows_per_batch, seq, rows):
    d = h.shape[1]
    lead = rows_per_batch - seq
    return pl.pallas_call(
        _final_norm_kernel,
        out_shape=jax.ShapeDtypeStruct((batch, seq, d), F32),
        grid=(batch, seq // rows),
        in_specs=[pl.BlockSpec((pl.Element(1), pl.Element(rows), pl.Element(d)),
                               lambda b, j: (b, pl.multiple_of(lead + rows * j, N_META), 0)),
                  pl.BlockSpec((1, d), lambda b, j: (0, 0))],
        out_specs=pl.BlockSpec((1, rows, d), lambda b, j: (b, j, 0)),
        compiler_params=_params(("parallel", "parallel")),
        name="final_norm",
    )(h.reshape(batch, rows_per_batch, d), g.reshape(1, d))


def _mm_res_kernel(a_ref, w_ref, r_ref, o_ref):
    o_ref[...] = r_ref[...] + _dot(a_ref[...], w_ref[...])


def _matmul_res(a, w, residual, *, tm, tn, name):
    m, k = a.shape
    n = w.shape[1]
    return pl.pallas_call(
        _mm_res_kernel,
        out_shape=jax.ShapeDtypeStruct((m, n), F32),
        grid=(m // tm, n // tn),
        in_specs=[pl.BlockSpec((tm, k), lambda i, j: (i, 0)),
                  pl.BlockSpec((k, tn), lambda i, j: (0, j)),
                  pl.BlockSpec((tm, tn), lambda i, j: (i, j))],
        out_specs=pl.BlockSpec((tm, tn), lambda i, j: (i, j)),
        compiler_params=_params(("parallel", "parallel")),
        name=name,
    )(a, w, residual)


def _out_proj_kernel(a_ref, w_ref, x_ref, meta_ref, o_ref):
    acc = _dot(a_ref[...], w_ref[...].astype(BF16))
    o_ref[0:N_META, :] = meta_ref[...] + acc[0:N_META]
    o_ref[N_META:, :] = x_ref[0] + acc[N_META:]


def _out_proj(a, w, x, meta_tokens, *, tn):
    batch, seq, d = x.shape
    m, k = a.shape
    rows = N_META + seq
    return pl.pallas_call(
        _out_proj_kernel,
        out_shape=jax.ShapeDtypeStruct((m, d), F32),
        grid=(batch, d // tn),
        in_specs=[pl.BlockSpec((rows, k), lambda b, j: (b, 0), pipeline_mode=pl.Buffered(1)),
                  pl.BlockSpec((k, tn), lambda b, j: (0, j)),
                  pl.BlockSpec((1, seq, tn), lambda b, j: (b, 0, j)),
                  pl.BlockSpec((N_META, tn), lambda b, j: (0, j))],
        out_specs=pl.BlockSpec((rows, tn), lambda b, j: (b, j)),
        compiler_params=_params(("parallel", "parallel")),
        name="out_proj",
    )(a, w, x, meta_tokens)


def _in_proj_kernel(a_ref, wt_ref, wz_ref, o_ref, z_ref):
    o_ref[...] = _dot_nt(a_ref[...], wt_ref[...].astype(BF16)).astype(o_ref.dtype)

    @pl.when(pl.program_id(1) == 0)
    def _():
        z_ref[...] = _dot_nt(a_ref[...], wz_ref[...].astype(BF16))


def _in_proj(a, w_in_t, *, tm, tn):
    m, k = a.shape

    def row_start(j):
        return pl.multiple_of(j * tn + jnp.where(j * tn >= N_MAIN, GLA_RANK, 0), GLA_RANK)

    return pl.pallas_call(
        _in_proj_kernel,
        out_shape=(jax.ShapeDtypeStruct((m, N_PROJ), BF16), jax.ShapeDtypeStruct((m, GLA_RANK), F32)),
        grid=(m // tm, N_PROJ // tn),
        in_specs=[pl.BlockSpec((tm, k), lambda i, j: (i, 0), pipeline_mode=pl.Buffered(1)),
                  pl.BlockSpec((pl.Element(tn), pl.Element(k)), lambda i, j: (row_start(j), 0)),
                  pl.BlockSpec((GLA_RANK, k), lambda i, j: (N_MAIN // GLA_RANK, 0))],
        out_specs=(pl.BlockSpec((tm, tn), lambda i, j: (i, j)),
                   pl.BlockSpec((tm, GLA_RANK), lambda i, j: (i, 0))),
        compiler_params=_params(("parallel", "arbitrary")),
        name="in_proj",
    )(a, w_in_t, w_in_t)


def _rope_table_kernel(pos_ref, invf_ref, cos_ref, sin_ref):
    ang = pos_ref[...].astype(F32) * invf_ref[...]
    cos_ref[...] = jnp.cos(ang)
    sin_ref[...] = jnp.sin(ang)


def _rope_tables(pos_col, inv_freq, *, rows):
    m = pos_col.shape[0]
    half = inv_freq.shape[1]
    out = jax.ShapeDtypeStruct((m, half), F32)
    return pl.pallas_call(
        _rope_table_kernel,
        out_shape=(out, out),
        grid=(m // rows,),
        in_specs=[pl.BlockSpec((rows, 1), lambda i: (i, 0)),
                  pl.BlockSpec((1, half), lambda i: (0, 0))],
        out_specs=(pl.BlockSpec((rows, half), lambda i: (i, 0)),
                   pl.BlockSpec((rows, half), lambda i: (i, 0))),
        compiler_params=_params(("parallel",)),
        name="rope_tables",
    )(pos_col, inv_freq)


def _lower_tri(n):
    return (_iota((n, n), 0) >= _iota((n, n), 1)).astype(BF16)


def _gla_log_decay(rows, tri, z_ref, wgu_ref, bg_ref):
    u = _dot(z_ref[rows, :].astype(BF16), wgu_ref[...]) + bg_ref[...]
    log_a = (jnp.minimum(u, 0.0) - jnp.log(1.0 + jnp.exp(-jnp.abs(u)))) * (1.0 / GLA_GATE_TAU)
    hi = log_a.astype(BF16)
    r1 = log_a - hi.astype(F32)
    mid = r1.astype(BF16)
    lo = (r1 - mid.astype(F32)).astype(BF16)
    return _dot(tri, hi) + _dot(tri, mid) + _dot(tri, lo)


def _gla_finish(rows, y, q, k, v, b, g_ref, out_ref, s_ref):
    n = b.shape[0]
    b_last = b[n - 1:n, :]
    kd = k * jnp.exp(b_last - b)
    upd = _dot_tn(kd.astype(BF16), v)
    decay_col = jnp.broadcast_to(jnp.exp(b_last), (LANES, GLA_DK)).T
    s_ref[...] = s_ref[...] * jnp.tile(decay_col, (1, GLA_DV // LANES)) + upd
    yn = y * lax.rsqrt(jnp.mean(y * y, axis=-1, keepdims=True) + EPS) * g_ref[...]
    out_ref[rows, :] = yn.astype(out_ref.dtype)


def _gla_chunk_direct(r0, n, b, q_ref, k_ref, v_ref, g_ref, out_ref, s_ref):
    rows = pl.ds(r0, n)
    q = q_ref[rows, :].astype(F32) * (GLA_DK ** -0.5)
    k = k_ref[rows, :].astype(F32)
    v = v_ref[rows, :]
    qe = (q * jnp.exp(b)).astype(BF16)
    cross = _dot(qe, s_ref[...].astype(BF16))
    scores = _dot_nt(qe, (k * jnp.exp(-b)).astype(BF16))
    causal = _iota((n, n), 0) >= _iota((n, n), 1)
    y = _dot(jnp.where(causal, scores, 0.0).astype(BF16), v) + cross
    _gla_finish(rows, y, q, k, v, b, g_ref, out_ref, s_ref)


def _gla_chunk_exact(r0, c_len, q_ref, k_ref, v_ref, z_ref, wgu_ref, bg_ref, g_ref, out_ref, s_ref):
    sub = GLA_SUB
    rows = pl.ds(r0, c_len)
    b = _gla_log_decay(rows, _lower_tri(c_len), z_ref, wgu_ref, bg_ref)
    q = q_ref[rows, :].astype(F32) * (GLA_DK ** -0.5)
    k = k_ref[rows, :].astype(F32)
    v = v_ref[rows, :]
    cross = _dot((q * jnp.exp(b)).astype(BF16), s_ref[...].astype(BF16))

    key_row = _iota((c_len, 1), 0)
    lane = _iota((sub, c_len), 1)
    qrow = _iota((sub, c_len), 0)
    blocks = []
    for s in range(c_len // sub):
        sl = slice(sub * s, sub * (s + 1))
        bs, qs, ks = b[sl], q[sl], k[sl]
        acc = jnp.zeros((sub, c_len), F32)
        for j in range(sub):
            w = jnp.exp(jnp.minimum(bs - bs[j:j + 1, :], 0.0))
            col = jnp.sum(qs * w * ks[j:j + 1, :], axis=-1, keepdims=True)
            acc = jnp.where((lane == sub * s + j) & (qrow >= j), col, acc)
        if s > 0:
            b_ref_row = b[sub * s - 1:sub * s, :]
            qt = qs * jnp.exp(bs - b_ref_row)
            earlier = key_row < sub * s
            kt = jnp.where(earlier, k * jnp.exp(jnp.where(earlier, b_ref_row - b, 0.0)), 0.0)
            acc = acc + _dot_nt(qt.astype(BF16), kt.astype(BF16))
        blocks.append(acc)
    scores = jnp.concatenate(blocks, axis=0)
    y = _dot(scores.astype(BF16), v) + cross
    _gla_finish(rows, y, q, k, v, b, g_ref, out_ref, s_ref)


def _gla_chunk(r0, tri, q_ref, k_ref, v_ref, z_ref, wgu_ref, bg_ref, g_ref, out_ref, s_ref):
    n = tri.shape[0]
    b = _gla_log_decay(pl.ds(r0, n), tri, z_ref, wgu_ref, bg_ref)
    span = jnp.max(-b[n - 1:n, :])
    small = span <= GLA_SAFE_SPAN

    @pl.when(small)
    def _():
        _gla_chunk_direct(r0, n, b, q_ref, k_ref, v_ref, g_ref, out_ref, s_ref)

    @pl.when(jnp.logical_not(small))
    def _():
        c_len = min(n, GLA_EXACT_CHUNK)

        def body(i, carry):
            _gla_chunk_exact(pl.multiple_of(r0 + i * c_len, GLA_SUB), c_len,
                             q_ref, k_ref, v_ref, z_ref, wgu_ref, bg_ref, g_ref, out_ref, s_ref)
            return carry

        lax.fori_loop(0, n // c_len, body, 0)


def _gla_kernel(q_ref, k_ref, v_ref, z_ref, wgu_ref, bg_ref, g_ref, out_ref, s_ref):
    refs = (q_ref, k_ref, v_ref, z_ref, wgu_ref, bg_ref, g_ref, out_ref, s_ref)
    s_ref[...] = jnp.zeros_like(s_ref)
    _gla_chunk(0, _lower_tri(N_META), *refs)
    tri = _lower_tri(BIG_CHUNK)

    def body(c, carry):
        _gla_chunk(pl.multiple_of(N_META + c * BIG_CHUNK, N_META), tri, *refs)
        return carry

    lax.fori_loop(0, (q_ref.shape[0] - N_META) // BIG_CHUNK, body, 0)


def _gla(proj, z, wgu, bg, g_gla, *, batch, rows):
    m = proj.shape[0]
    return pl.pallas_call(
        _gla_kernel,
        out_shape=jax.ShapeDtypeStruct((m, D_MODEL), BF16),
        grid=(batch, GLA_HEADS),
        in_specs=[
            pl.BlockSpec((rows, GLA_DK), lambda b, h: (b, COL_Q_G // GLA_DK + h)),
            pl.BlockSpec((rows, GLA_DK), lambda b, h: (b, COL_K_G // GLA_DK + h)),
            pl.BlockSpec((rows, GLA_DV), lambda b, h: (b, COL_V_G // GLA_DV + h)),
            pl.BlockSpec((rows, GLA_RANK), lambda b, h: (b, 0)),
            pl.BlockSpec((GLA_RANK, GLA_DK), lambda b, h: (0, h)),
            pl.BlockSpec((1, GLA_DK), lambda b, h: (0, h)),
            pl.BlockSpec((1, GLA_DV), lambda b, h: (0, h)),
        ],
        out_specs=pl.BlockSpec((rows, GLA_DV), lambda b, h: (b, h)),
        scratch_shapes=[pltpu.VMEM((GLA_DK, GLA_DV), F32)],
        compiler_params=_params(("parallel", "parallel")),
        name="gla",
    )(proj, proj, proj, z, wgu, bg, g_gla.reshape(1, D_MODEL))


def _ret_decays(lg, n):
    rel = (_iota((n, n), 0) - _iota((n, n), 1)).astype(F32)
    d_intra = jnp.where(rel >= 0, jnp.exp(lg * jnp.maximum(rel, 0.0)), 0.0)
    ridx = _iota((n, 1), 0).astype(F32)
    d_q = jnp.exp(lg * (ridx + 1.0))
    d_k = jnp.exp(lg * (n - 1.0 - ridx))
    d_chunk = jnp.exp(lg * float(n))
    return d_intra, d_q, d_k, d_chunk


def _ret_chunk(r0, decays, q_ref, k_ref, v_ref, o_ref, m_ref, cos_ref, sin_ref, g_ref,
               yg_ref, og_ref, mg_ref, out_ref, s_ref):
    d_intra, d_q, d_k, d_chunk = decays
    n = d_intra.shape[0]
    rows = pl.ds(r0, n)
    half = RET_DK // 2
    cos = cos_ref[rows, :]
    sin = sin_ref[rows, :]

    def rope(ref):
        x = ref[rows, :].astype(F32)
        x1, x2 = x[:, :half], x[:, half:]
        return jnp.concatenate([x1 * cos - x2 * sin, x2 * cos + x1 * sin], axis=-1)

    q = rope(q_ref)
    k = rope(k_ref) * (RET_DK ** -0.5)
    v = v_ref[rows, :]

    qb = q.astype(BF16)
    scores = _dot_nt(qb, k.astype(BF16)) * d_intra
    state = s_ref[...]
    y = _dot(scores.astype(BF16), v) + _dot(qb, state.astype(BF16)) * d_q
    s_ref[...] = state * d_chunk + _dot_tn((k * d_k).astype(BF16), v)

    mu = jnp.mean(y, axis=-1, keepdims=True)
    yc = y - mu
    var = jnp.mean(yc * yc, axis=-1, keepdims=True)
    yn = yc * lax.rsqrt(var + EPS) * g_ref[...]

    def gated4(branch, o_gate_ref, m_gate_ref):
        o = o_gate_ref[rows, :].astype(F32)
        m = m_gate_ref[rows, :].astype(F32)
        return (1.0 + jnp.tanh(0.5 * m)) * (1.0 + jnp.tanh(0.5 * o)) * (o * branch)

    merged = 0.25 * (gated4(yn, o_ref, m_ref) + gated4(yg_ref[rows, :].astype(F32), og_ref, mg_ref))
    out_ref[rows, :] = merged.astype(out_ref.dtype)


def _ret_kernel(lg_ref, q_ref, k_ref, v_ref, o_ref, m_ref, cos_ref, sin_ref, g_ref,
                yg_ref, og_ref, mg_ref, out_ref, s_ref):
    lg = lg_ref[0][:, :1]
    refs = (q_ref, k_ref, v_ref, o_ref, m_ref, cos_ref, sin_ref, g_ref, yg_ref, og_ref, mg_ref,
            out_ref, s_ref)
    s_ref[...] = jnp.zeros_like(s_ref)
    _ret_chunk(0, _ret_decays(lg, N_META), *refs)
    decays = _ret_decays(lg, BIG_CHUNK)

    def body(c, carry):
        r0 = pl.multiple_of(N_META + c * BIG_CHUNK, N_META)
        _ret_chunk(r0, decays, *refs)
        return carry

    lax.fori_loop(0, (q_ref.shape[0] - N_META) // BIG_CHUNK, body, 0)


def _retention_merge(proj, y_gla, cos, sin, g_ret, *, batch, rows):
    m = proj.shape[0]
    log_gamma = jnp.log1p(-jnp.exp2(-5.0 - jnp.arange(RET_HEADS, dtype=F32)))
    lg = jnp.broadcast_to(log_gamma[:, None, None], (RET_HEADS, 1, LANES))

    def cols(start, width):
        return lambda b, h: (b, start // width + h)

    dk, dv = RET_DK, RET_DV
    return pl.pallas_call(
        _ret_kernel,
        out_shape=jax.ShapeDtypeStruct((m, D_MODEL), BF16),
        grid=(batch, RET_HEADS),
        in_specs=[
            pl.BlockSpec((1, 1, LANES), lambda b, h: (h, 0, 0)),
            pl.BlockSpec((rows, dk), cols(COL_Q_R, dk)),
            pl.BlockSpec((rows, dk), cols(COL_K_R, dk)),
            pl.BlockSpec((rows, dv), cols(COL_V_R, dv)),
            pl.BlockSpec((rows, dv), cols(COL_O_R, dv)),
            pl.BlockSpec((rows, dv), cols(COL_M_R, dv)),
            pl.BlockSpec((rows, dk // 2), lambda b, h: (b, 0)),
            pl.BlockSpec((rows, dk // 2), lambda b, h: (b, 0)),
            pl.BlockSpec((1, dv), lambda b, h: (0, h)),
            pl.BlockSpec((rows, dv), lambda b, h: (b, h)),
            pl.BlockSpec((rows, dv), cols(COL_O_G, dv)),
            pl.BlockSpec((rows, dv), cols(COL_M_G, dv)),
        ],
        out_specs=pl.BlockSpec((rows, dv), lambda b, h: (b, h)),
        scratch_shapes=[pltpu.VMEM((dk, dv), F32)],
        compiler_params=_params(("parallel", "parallel")),
        name="retention_merge",
    )(lg, proj, proj, proj, proj, proj, cos, sin, g_ret.reshape(1, D_MODEL), y_gla, proj, proj)


def _ffn_in_kernel(a_ref, wu_ref, wg_ref, cw_ref, cb_ref, wo_ref, hid_ref, wo_bf16_ref, up_ref):
    wo_bf16_ref[...] = wo_ref[...].astype(BF16)
    rows = a_ref.shape[0]
    tf = wu_ref.shape[1]
    w = jnp.concatenate([wu_ref[...].astype(BF16), wg_ref[...].astype(BF16)], axis=1)
    cw = 0.5 * cw_ref[...]
    cb = 0.5 * cb_ref[...]
    up_ref[0:8, :] = jnp.zeros((8, tf), F32)
    starts = list(range(0, rows - FFN_ROW_CHUNK + 1, FFN_ROW_CHUNK))
    for lo, hi in zip(starts, starts[1:] + [rows]):
        up_gate = _dot(a_ref[lo:hi, :], w)
        up, gate = up_gate[:, :tf], up_gate[:, tf:]
        up_ref[8 + lo:8 + hi, :] = up
        ch = (cb + cw[0:1, :] * up_ref[6 + lo:6 + hi, :] + cw[1:2, :] * up_ref[7 + lo:7 + hi, :]
              + cw[2:3, :] * up)
        hid_ref[lo:hi, :] = (ch * (1.0 + jnp.tanh(ch)) * gate).astype(hid_ref.dtype)


def _ffn_in(a, w_ffn_in, conv_w, conv_b, w_ffn_out, *, rows, tf):
    m, k = a.shape
    nf = D_FF // tf
    n_steps = (m // rows) * nf
    d_out = w_ffn_out.shape[1]
    slab = D_FF // n_steps
    assert slab * n_steps == D_FF and slab % 16 == 0
    return pl.pallas_call(
        _ffn_in_kernel,
        out_shape=(jax.ShapeDtypeStruct((m, D_FF), BF16),
                   jax.ShapeDtypeStruct((D_FF, d_out), BF16)),
        grid=(m // rows, nf),
        in_specs=[
            pl.BlockSpec((rows, k), lambda i, j: (i, 0), pipeline_mode=pl.Buffered(1)),
            pl.BlockSpec((k, tf), lambda i, j: (0, j)),
            pl.BlockSpec((k, tf), lambda i, j: (0, nf + j)),
            pl.BlockSpec((CONV_W, tf), lambda i, j: (0, j)),
            pl.BlockSpec((1, tf), lambda i, j: (0, j)),
            pl.BlockSpec((slab, d_out), lambda i, j: (i * nf + j, 0)),
        ],
        out_specs=(pl.BlockSpec((rows, tf), lambda i, j: (i, j)),
                   pl.BlockSpec((slab, d_out), lambda i, j: (i * nf + j, 0))),
        scratch_shapes=[pltpu.VMEM((rows + 8, tf), F32)],
        compiler_params=_params(("parallel", "parallel")),
        name="ffn_in_conv",
    )(a, w_ffn_in, w_ffn_in, conv_w, conv_b.reshape(1, D_FF), w_ffn_out)


def kernel(x, positions, meta_tokens, attn_norm, w_in, w_gate_up, b_gate, ret_norm, gla_norm,
           w_out, ffn_norm, w_ffn_in, conv_w, conv_b, w_ffn_out, final_norm):
    batch, seq, d = x.shape
    rows = N_META + seq
    m = batch * rows

    meta_tokens = meta_tokens.astype(x.dtype)
    pos = jnp.concatenate([
        jnp.broadcast_to(jnp.arange(N_META, dtype=jnp.int32), (batch, N_META)),
        positions.astype(jnp.int32) + N_META], axis=1).reshape(m, 1)
    half = RET_DK // 2
    inv_freq = (ROPE_BASE ** (-jnp.arange(half, dtype=F32) / half)).reshape(1, half)

    w_in_t = jnp.swapaxes(w_in, 1, 2)[0]

    cos, sin = _rope_tables(pos, inv_freq, rows=rows)

    norm_rows = rows // 3
    hn = _embed_norm(x, meta_tokens, attn_norm[0], rows=norm_rows, out_dtype=BF16)
    proj, z = _in_proj(hn, w_in_t, tm=rows, tn=512)

    y_gla = _gla(proj, z, w_gate_up[0].astype(BF16), b_gate[0].reshape(1, -1), gla_norm[0],
                 batch=batch, rows=rows)
    merged = _retention_merge(proj, y_gla, cos, sin, ret_norm[0], batch=batch, rows=rows)

    h1 = _out_proj(merged, w_out[0], x, meta_tokens, tn=256)
    h1n = _rmsnorm(h1, ffn_norm[0], rows=norm_rows, out_dtype=BF16)
    hidden, w_ffn_out_b = _ffn_in(h1n, w_ffn_in[0], conv_w[0], conv_b[0], w_ffn_out[0],
                                  rows=rows, tf=256)
    h2 = _matmul_res(hidden, w_ffn_out_b, h1, tm=norm_rows, tn=256, name="ffn_out")
    return _final_norm(h2, final_norm, batch=batch, rows_per_batch=rows, seq=seq, rows=256)
```

```python
import jax
import jax.numpy as jnp
from jax import lax
from jax.experimental import pallas as pl
from jax.experimental.pallas import tpu as pltpu

F32 = jnp.float32
BF16 = jnp.bfloat16

D_MODEL = 4096
N_META = 16
RET_HEADS, RET_DK, RET_DV = 8, 256, 512
GLA_HEADS, GLA_DK, GLA_DV = 4, 512, 1024
GLA_RANK = 16
GLA_GATE_TAU = 16.0
D_FF = 11008
CONV_W = 3
ROPE_BASE = 10000.0
EPS = 1e-6
RET_QK = RET_HEADS * RET_DK
GLA_QK = GLA_HEADS * GLA_DK
COL_Q_R, COL_K_R = 0, RET_QK
COL_V_R, COL_O_R = 2 * RET_QK, 2 * RET_QK + D_MODEL
COL_Q_G = 2 * RET_QK + 2 * D_MODEL
COL_K_G, COL_V_G = COL_Q_G + GLA_QK, COL_Q_G + 2 * GLA_QK
COL_O_G = COL_V_G + D_MODEL
N_MAIN = COL_O_G + D_MODEL
COL_M_R, COL_M_G = N_MAIN, N_MAIN + D_MODEL
N_PROJ = N_MAIN + 2 * D_MODEL

VMEM_LIMIT_V7X = 56 * 1024 * 1024
LANES = 128
BIG_CHUNK = 256
FFN_ROW_CHUNK = 256
GLA_EXACT_CHUNK = 64
GLA_SUB = 16
GLA_SAFE_SPAN = 60.0


def _params(sem):
    return pltpu.CompilerParams(dimension_semantics=sem, vmem_limit_bytes=VMEM_LIMIT_V7X)


def _iota(shape, dim):
    return lax.broadcasted_iota(jnp.int32, shape, dim)


def _dot(a, b):
    return jnp.dot(a, b, preferred_element_type=F32)


def _dot_nt(a, b):
    return lax.dot_general(a, b, (((1,), (1,)), ((), ())), preferred_element_type=F32)


def _dot_tn(a, b):
    return lax.dot_general(a, b, (((0,), (0,)), ((), ())), preferred_element_type=F32)


def _rmsnorm_kernel(x_ref, g_ref, o_ref):
    x = x_ref[...]
    y = x * lax.rsqrt(jnp.mean(x * x, axis=-1, keepdims=True) + EPS)
    o_ref[...] = (y * g_ref[...]).astype(o_ref.dtype)


def _rmsnorm(x, g, *, rows, out_dtype):
    m, d = x.shape
    return pl.pallas_call(
        _rmsnorm_kernel,
        out_shape=jax.ShapeDtypeStruct((m, d), out_dtype),
        grid=(m // rows,),
        in_specs=[pl.BlockSpec((rows, d), lambda i: (i, 0)),
                  pl.BlockSpec((1, d), lambda i: (0, 0))],
        out_specs=pl.BlockSpec((rows, d), lambda i: (i, 0)),
        compiler_params=_params(("parallel",)),
        name="rmsnorm",
    )(x, g.reshape(1, d))


def _embed_norm_kernel(x_ref, meta_ref, g_ref, o_ref):
    def norm(v):
        y = v * lax.rsqrt(jnp.mean(v * v, axis=-1, keepdims=True) + EPS)
        return (y * g_ref[...]).astype(o_ref.dtype)

    r = pl.program_id(1)
    rows = o_ref.shape[0]

    @pl.when(r == 0)
    def _():
        o_ref[0:N_META, :] = norm(meta_ref[...])
        o_ref[N_META:, :] = norm(x_ref[0, 0:rows - N_META, :])

    @pl.when(r > 0)
    def _():
        o_ref[...] = norm(x_ref[0])


def _embed_norm(x, meta_tokens, g, *, rows, out_dtype):
    batch, seq, d = x.shape
    tiles = (N_META + seq) // rows

    def x_start(r):
        return pl.multiple_of(jnp.maximum(rows * r - N_META, 0), N_META)

    return pl.pallas_call(
        _embed_norm_kernel,
        out_shape=jax.ShapeDtypeStruct((batch * (N_META + seq), d), out_dtype),
        grid=(batch, tiles),
        in_specs=[pl.BlockSpec((pl.Element(1), pl.Element(rows), pl.Element(d)),
                               lambda b, r: (b, x_start(r), 0)),
                  pl.BlockSpec((N_META, d), lambda b, r: (0, 0)),
                  pl.BlockSpec((1, d), lambda b, r: (0, 0))],
        out_specs=pl.BlockSpec((rows, d), lambda b, r: (b * tiles + r, 0)),
        compiler_params=_params(("parallel", "parallel")),
        name="embed_norm",
    )(x, meta_tokens, g.reshape(1, d))


def _final_norm_kernel(x_ref, g_ref, o_ref):
    x = x_ref[...]
    y = x * lax.rsqrt(jnp.mean(x * x, axis=-1, keepdims=True) + EPS)
    o_ref[...] = y * g_ref[...]


def _final_norm(h, g, *, batch, rows_per_batch, seq, rows):
    d = h.shape[1]
    lead = rows_per_batch - seq
    return pl.pallas_call(
        _final_norm_kernel,
        out_shape=jax.ShapeDtypeStruct((batch, seq, d), F32),
        grid=(batch, seq // rows),
        in_specs=[pl.BlockSpec((pl.Element(1), pl.Element(rows), pl.Element(d)),
                               lambda b, j: (b, pl.multiple_of(lead + rows * j, N_META), 0)),
                  pl.BlockSpec((1, d), lambda b, j: (0, 0))],
        out_specs=pl.BlockSpec((1, rows, d), lambda b, j: (b, j, 0)),
        compiler_params=_params(("parallel", "parallel")),
        name="final_norm",
    )(h.reshape(batch, rows_per_batch, d), g.reshape(1, d))


def _mm_res_kernel(a_ref, w_ref, r_ref, o_ref):
    o_ref[...] = r_ref[...] + _dot(a_ref[...], w_ref[...])


def _matmul_res(a, w, residual, *, tm, tn, name):
    m, k = a.shape
    n = w.shape[1]
    return pl.pallas_call(
        _mm_res_kernel,
        out_shape=jax.ShapeDtypeStruct((m, n), F32),
        grid=(m // tm, n // tn),
        in_specs=[pl.BlockSpec((tm, k), lambda i, j: (i, 0)),
                  pl.BlockSpec((k, tn), lambda i, j: (0, j)),
                  pl.BlockSpec((tm, tn), lambda i, j: (i, j))],
        out_specs=pl.BlockSpec((tm, tn), lambda i, j: (i, j)),
        compiler_params=_params(("parallel", "parallel")),
        name=name,
    )(a, w, residual)


def _out_proj_kernel(a_ref, w_ref, x_ref, meta_ref, o_ref):
    acc = _dot(a_ref[...], w_ref[...].astype(BF16))
    o_ref[0:N_META, :] = meta_ref[...] + acc[0:N_META]
    o_ref[N_META:, :] = x_ref[0] + acc[N_META:]


def _out_proj(a, w, x, meta_tokens, *, tn):
    batch, seq, d = x.shape
    m, k = a.shape
    rows = N_META + seq
    return pl.pallas_call(
        _out_proj_kernel,
        out_shape=jax.ShapeDtypeStruct((m, d), F32),
        grid=(batch, d // tn),
        in_specs=[pl.BlockSpec((rows, k), lambda b, j: (b, 0), pipeline_mode=pl.Buffered(1)),
                  pl.BlockSpec((k, tn), lambda b, j: (0, j)),
                  pl.BlockSpec((1, seq, tn), lambda b, j: (b, 0, j)),
                  pl.BlockSpec((N_META, tn), lambda b, j: (0, j))],
        out_specs=pl.BlockSpec((rows, tn), lambda b, j: (b, j)),
        compiler_params=_params(("parallel", "parallel")),
        name="out_proj",
    )(a, w, x, meta_tokens)


def _in_proj_kernel(a_ref, wt_ref, wz_ref, o_ref, z_ref):
    o_ref[...] = _dot_nt(a_ref[...], wt_ref[...].astype(BF16)).astype(o_ref.dtype)

    @pl.when(pl.program_id(1) == 0)
    def _():
        z_ref[...] = _dot_nt(a_ref[...], wz_ref[...].astype(BF16))


def _in_proj(a, w_in_t, *, tm, tn):
    m, k = a.shape

    def row_start(j):
        return pl.multiple_of(j * tn + jnp.where(j * tn >= N_MAIN, GLA_RANK, 0), GLA_RANK)

    return pl.pallas_call(
        _in_proj_kernel,
        out_shape=(jax.ShapeDtypeStruct((m, N_PROJ), BF16), jax.ShapeDtypeStruct((m, GLA_RANK), F32)),
        grid=(m // tm, N_PROJ // tn),
        in_specs=[pl.BlockSpec((tm, k), lambda i, j: (i, 0), pipeline_mode=pl.Buffered(1)),
                  pl.BlockSpec((pl.Element(tn), pl.Element(k)), lambda i, j: (row_start(j), 0)),
                  pl.BlockSpec((GLA_RANK, k), lambda i, j: (N_MAIN // GLA_RANK, 0))],
        out_specs=(pl.BlockSpec((tm, tn), lambda i, j: (i, j)),
                   pl.BlockSpec((tm, GLA_RANK), lambda i, j: (i, 0))),
        compiler_params=_params(("parallel", "arbitrary")),
        name="in_proj",
    )(a, w_in_t, w_in_t)


def _rope_table_kernel(pos_ref, invf_ref, cos_ref, sin_ref):
    ang = pos_ref[...].astype(F32) * invf_ref[...]
    cos_ref[...] = jnp.cos(ang)
    sin_ref[...] = jnp.sin(ang)


def _rope_tables(pos_col, inv_freq, *, rows):
    m = pos_col.shape[0]
    half = inv_freq.shape[1]
    out = jax.ShapeDtypeStruct((m, half), F32)
    return pl.pallas_call(
        _rope_table_kernel,
        out_shape=(out, out),
        grid=(m // rows,),
        in_specs=[pl.BlockSpec((rows, 1), lambda i: (i, 0)),
                  pl.BlockSpec((1, half), lambda i: (0, 0))],
        out_specs=(pl.BlockSpec((rows, half), lambda i: (i, 0)),
                   pl.BlockSpec((rows, half), lambda i: (i, 0))),
        compiler_params=_params(("parallel",)),
        name="rope_tables",
    )(pos_col, inv_freq)


def _lower_tri(n):
    return (_iota((n, n), 0) >= _iota((n, n), 1)).astype(BF16)


def _gla_log_decay(rows, tri, z_ref, wgu_ref, bg_ref):
    u = _dot(z_ref[rows, :].astype(BF16), wgu_ref[...]) + bg_ref[...]
    log_a = (jnp.minimum(u, 0.0) - jnp.log(1.0 + jnp.exp(-jnp.abs(u)))) * (1.0 / GLA_GATE_TAU)
    hi = log_a.astype(BF16)
    r1 = log_a - hi.astype(F32)
    mid = r1.astype(BF16)
    lo = (r1 - mid.astype(F32)).astype(BF16)
    return _dot(tri, hi) + _dot(tri, mid) + _dot(tri, lo)


def _gla_finish(rows, y, q, k, v, b, g_ref, out_ref, s_ref):
    n = b.shape[0]
    b_last = b[n - 1:n, :]
    kd = k * jnp.exp(b_last - b)
    upd = _dot_tn(kd.astype(BF16), v)
    decay_col = jnp.broadcast_to(jnp.exp(b_last), (LANES, GLA_DK)).T
    s_ref[...] = s_ref[...] * jnp.tile(decay_col, (1, GLA_DV // LANES)) + upd
    yn = y * lax.rsqrt(jnp.mean(y * y, axis=-1, keepdims=True) + EPS) * g_ref[...]
    out_ref[rows, :] = yn.astype(out_ref.dtype)


def _gla_chunk_direct(r0, n, b, q_ref, k_ref, v_ref, g_ref, out_ref, s_ref):
    rows = pl.ds(r0, n)
    q = q_ref[rows, :].astype(F32) * (GLA_DK ** -0.5)
    k = k_ref[rows, :].astype(F32)
    v = v_ref[rows, :]
    qe = (q * jnp.exp(b)).astype(BF16)
    cross = _dot(qe, s_ref[...].astype(BF16))
    scores = _dot_nt(qe, (k * jnp.exp(-b)).astype(BF16))
    causal = _iota((n, n), 0) >= _iota((n, n), 1)
    y = _dot(jnp.where(causal, scores, 0.0).astype(BF16), v) + cross
    _gla_finish(rows, y, q, k, v, b, g_ref, out_ref, s_ref)


def _gla_chunk_exact(r0, c_len, q_ref, k_ref, v_ref, z_ref, wgu_ref, bg_ref, g_ref, out_ref, s_ref):
    sub = GLA_SUB
    rows = pl.ds(r0, c_len)
    b = _gla_log_decay(rows, _lower_tri(c_len), z_ref, wgu_ref, bg_ref)
    q = q_ref[rows, :].astype(F32) * (GLA_DK ** -0.5)
    k = k_ref[rows, :].astype(F32)
    v = v_ref[rows, :]
    cross = _dot((q * jnp.exp(b)).astype(BF16), s_ref[...].astype(BF16))

    key_row = _iota((c_len, 1), 0)
    lane = _iota((sub, c_len), 1)
    qrow = _iota((sub, c_len), 0)
    blocks = []
    for s in range(c_len // sub):
        sl = slice(sub * s, sub * (s + 1))
        bs, qs, ks = b[sl], q[sl], k[sl]
        acc = jnp.zeros((sub, c_len), F32)
        for j in range(sub):
            w = jnp.exp(jnp.minimum(bs - bs[j:j + 1, :], 0.0))
            col = jnp.sum(qs * w * ks[j:j + 1, :], axis=-1, keepdims=True)
            acc = jnp.where((lane == sub * s + j) & (qrow >= j), col, acc)
        if s > 0:
            b_ref_row = b[sub * s - 1:sub * s, :]
            qt = qs * jnp.exp(bs - b_ref_row)
            earlier = key_row < sub * s
            kt = jnp.where(earlier, k * jnp.exp(jnp.where(earlier, b_ref_row - b, 0.0)), 0.0)
            acc = acc + _dot_nt(qt.astype(BF16), kt.astype(BF16))
        blocks.append(acc)
    scores = jnp.concatenate(blocks, axis=0)
    y = _dot(scores.astype(BF16), v) + cross
    _gla_finish(rows, y, q, k, v, b, g_ref, out_ref, s_ref)


def _chunk_span(b):
    n = b.shape[0]
    return jnp.max(-b[n - 1:n, :], axis=1, keepdims=True)


def _gla_chunk_any(r0, n, q_ref, k_ref, v_ref, z_ref, wgu_ref, bg_ref, g_ref, out_ref, s_ref, b_ref):
    b = b_ref[pl.ds(r0, n), :]
    small = _chunk_span(b)[0, 0] <= GLA_SAFE_SPAN

    @pl.when(small)
    def _():
        _gla_chunk_direct(r0, n, b, q_ref, k_ref, v_ref, g_ref, out_ref, s_ref)

    @pl.when(jnp.logical_not(small))
    def _():
        c_len = min(n, GLA_EXACT_CHUNK)

        def body(i, carry):
            _gla_chunk_exact(pl.multiple_of(r0 + i * c_len, GLA_SUB), c_len,
                             q_ref, k_ref, v_ref, z_ref, wgu_ref, bg_ref, g_ref, out_ref, s_ref)
            return carry

        lax.fori_loop(0, n // c_len, body, 0)


def _gla_kernel(q_ref, k_ref, v_ref, z_ref, wgu_ref, bg_ref, g_ref, out_ref, s_ref, b_ref):
    refs = (q_ref, k_ref, v_ref, z_ref, wgu_ref, bg_ref, g_ref, out_ref, s_ref)
    n_big = (q_ref.shape[0] - N_META) // BIG_CHUNK

    def big_start(c):
        return pl.multiple_of(N_META + c * BIG_CHUNK, N_META)

    b_meta = _gla_log_decay(pl.ds(0, N_META), _lower_tri(N_META), z_ref, wgu_ref, bg_ref)
    b_ref[0:N_META, :] = b_meta
    tri = _lower_tri(BIG_CHUNK)

    def decay_body(c, span):
        rows = pl.ds(big_start(c), BIG_CHUNK)
        b = _gla_log_decay(rows, tri, z_ref, wgu_ref, bg_ref)
        b_ref[rows, :] = b
        return jnp.maximum(span, _chunk_span(b))

    span = lax.fori_loop(0, n_big, decay_body, _chunk_span(b_meta), unroll=True)
    all_small = span[0, 0] <= GLA_SAFE_SPAN

    s_ref[...] = jnp.zeros_like(s_ref)

    @pl.when(all_small)
    def _():
        direct_refs = (q_ref, k_ref, v_ref, g_ref, out_ref, s_ref)
        _gla_chunk_direct(0, N_META, b_ref[0:N_META, :], *direct_refs)

        def body(c, carry):
            r0 = big_start(c)
            _gla_chunk_direct(r0, BIG_CHUNK, b_ref[pl.ds(r0, BIG_CHUNK), :], *direct_refs)
            return carry

        lax.fori_loop(0, n_big, body, 0)

    @pl.when(jnp.logical_not(all_small))
    def _():
        _gla_chunk_any(0, N_META, *refs, b_ref)

        def body(c, carry):
            _gla_chunk_any(big_start(c), BIG_CHUNK, *refs, b_ref)
            return carry

        lax.fori_loop(0, n_big, body, 0)


def _gla(proj, z, wgu, bg, g_gla, *, batch, rows):
    m = proj.shape[0]
    return pl.pallas_call(
        _gla_kernel,
        out_shape=jax.ShapeDtypeStruct((m, D_MODEL), BF16),
        grid=(batch, GLA_HEADS),
        in_specs=[
            pl.BlockSpec((rows, GLA_DK), lambda b, h: (b, COL_Q_G // GLA_DK + h)),
            pl.BlockSpec((rows, GLA_DK), lambda b, h: (b, COL_K_G // GLA_DK + h)),
            pl.BlockSpec((rows, GLA_DV), lambda b, h: (b, COL_V_G // GLA_DV + h)),
            pl.BlockSpec((rows, GLA_RANK), lambda b, h: (b, 0)),
            pl.BlockSpec((GLA_RANK, GLA_DK), lambda b, h: (0, h)),
            pl.BlockSpec((1, GLA_DK), lambda b, h: (0, h)),
            pl.BlockSpec((1, GLA_DV), lambda b, h: (0, h)),
        ],
        out_specs=pl.BlockSpec((rows, GLA_DV), lambda b, h: (b, h)),
        scratch_shapes=[pltpu.VMEM((GLA_DK, GLA_DV), F32), pltpu.VMEM((rows, GLA_DK), F32)],
        compiler_params=_params(("parallel", "parallel")),
        name="gla",
    )(proj, proj, proj, z, wgu, bg, g_gla.reshape(1, D_MODEL))


def _ret_decays(lg, n):
    rel = (_iota((n, n), 0) - _iota((n, n), 1)).astype(F32)
    d_intra = jnp.where(rel >= 0, jnp.exp(lg * jnp.maximum(rel, 0.0)), 0.0)
    ridx = _iota((n, 1), 0).astype(F32)
    d_q = jnp.exp(lg * (ridx + 1.0))
    d_k = jnp.exp(lg * (n - 1.0 - ridx))
    d_chunk = jnp.exp(lg * float(n))
    return d_intra, d_q, d_k, d_chunk


def _ret_chunk(r0, decays, q_ref, k_ref, v_ref, o_ref, m_ref, cos_ref, sin_ref, g_ref,
               yg_ref, og_ref, mg_ref, out_ref, s_ref):
    d_intra, d_q, d_k, d_chunk = decays
    n = d_intra.shape[0]
    rows = pl.ds(r0, n)
    half = RET_DK // 2
    cos = cos_ref[rows, :]
    sin = sin_ref[rows, :]

    def rope(ref):
        x = ref[rows, :].astype(F32)
        x1, x2 = x[:, :half], x[:, half:]
        return jnp.concatenate([x1 * cos - x2 * sin, x2 * cos + x1 * sin], axis=-1)

    q = rope(q_ref)
    k = rope(k_ref) * (RET_DK ** -0.5)
    v = v_ref[rows, :]

    qb = q.astype(BF16)
    scores = _dot_nt(qb, k.astype(BF16)) * d_intra
    state = s_ref[...]
    y = _dot(scores.astype(BF16), v) + _dot(qb, state.astype(BF16)) * d_q
    s_ref[...] = state * d_chunk + _dot_tn((k * d_k).astype(BF16), v)

    mu = jnp.mean(y, axis=-1, keepdims=True)
    yc = y - mu
    var = jnp.mean(yc * yc, axis=-1, keepdims=True)
    yn = yc * lax.rsqrt(var + EPS) * g_ref[...]

    def gated4(branch, o_gate_ref, m_gate_ref):
        o = o_gate_ref[rows, :]
        m = m_gate_ref[rows, :]
        return (1.0 + jnp.tanh(0.5 * m)) * (1.0 + jnp.tanh(0.5 * o)) * (o * branch)

    merged = 0.25 * (gated4(yn.astype(BF16), o_ref, m_ref) + gated4(yg_ref[rows, :], og_ref, mg_ref))
    out_ref[rows, :] = merged.astype(out_ref.dtype)


def _ret_kernel(lg_ref, q_ref, k_ref, v_ref, o_ref, m_ref, cos_ref, sin_ref, g_ref,
                yg_ref, og_ref, mg_ref, out_ref, s_ref):
    lg = lg_ref[0][:, :1]
    refs = (q_ref, k_ref, v_ref, o_ref, m_ref, cos_ref, sin_ref, g_ref, yg_ref, og_ref, mg_ref,
            out_ref, s_ref)
    s_ref[...] = jnp.zeros_like(s_ref)
    _ret_chunk(0, _ret_decays(lg, N_META), *refs)
    decays = _ret_decays(lg, BIG_CHUNK)

    def body(c, carry):
        r0 = pl.multiple_of(N_META + c * BIG_CHUNK, N_META)
        _ret_chunk(r0, decays, *refs)
        return carry

    lax.fori_loop(0, (q_ref.shape[0] - N_META) // BIG_CHUNK, body, 0)


def _retention_merge(proj, y_gla, cos, sin, g_ret, *, batch, rows):
    m = proj.shape[0]
    log_gamma = jnp.log1p(-jnp.exp2(-5.0 - jnp.arange(RET_HEADS, dtype=F32)))
    lg = jnp.broadcast_to(log_gamma[:, None, None], (RET_HEADS, 1, LANES))

    def cols(start, width):
        return lambda b, h: (b, start // width + h)

    dk, dv = RET_DK, RET_DV
    return pl.pallas_call(
        _ret_kernel,
        out_shape=jax.ShapeDtypeStruct((m, D_MODEL), BF16),
        grid=(batch, RET_HEADS),
        in_specs=[
            pl.BlockSpec((1, 1, LANES), lambda b, h: (h, 0, 0)),
            pl.BlockSpec((rows, dk), cols(COL_Q_R, dk)),
            pl.BlockSpec((rows, dk), cols(COL_K_R, dk)),
            pl.BlockSpec((rows, dv), cols(COL_V_R, dv)),
            pl.BlockSpec((rows, dv), cols(COL_O_R, dv)),
            pl.BlockSpec((rows, dv), cols(COL_M_R, dv)),
            pl.BlockSpec((rows, dk // 2), lambda b, h: (b, 0)),
            pl.BlockSpec((rows, dk // 2), lambda b, h: (b, 0)),
            pl.BlockSpec((1, dv), lambda b, h: (0, h)),
            pl.BlockSpec((rows, dv), lambda b, h: (b, h)),
            pl.BlockSpec((rows, dv), cols(COL_O_G, dv)),
            pl.BlockSpec((rows, dv), cols(COL_M_G, dv)),
        ],
        out_specs=pl.BlockSpec((rows, dv), lambda b, h: (b, h)),
        scratch_shapes=[pltpu.VMEM((dk, dv), F32)],
        compiler_params=_params(("parallel", "parallel")),
        name="retention_merge",
    )(lg, proj, proj, proj, proj, proj, cos, sin, g_ret.reshape(1, D_MODEL), y_gla, proj, proj)


def _ffn_in_kernel(a_ref, wu_ref, wg_ref, cw_ref, cb_ref, wo_ref, hid_ref, wo_bf16_ref, up_ref):
    wo_bf16_ref[...] = wo_ref[...].astype(BF16)
    rows = a_ref.shape[0]
    tf = wu_ref.shape[1]
    w = jnp.concatenate([wu_ref[...].astype(BF16), wg_ref[...].astype(BF16)], axis=1)
    cw = 0.5 * cw_ref[...]
    cb = 0.5 * cb_ref[...]
    up_ref[0:8, :] = jnp.zeros((8, tf), F32)
    starts = list(range(0, rows - FFN_ROW_CHUNK + 1, FFN_ROW_CHUNK))
    for lo, hi in zip(starts, starts[1:] + [rows]):
        up_gate = _dot(a_ref[lo:hi, :], w)
        up, gate = up_gate[:, :tf], up_gate[:, tf:]
        up_ref[8 + lo:8 + hi, :] = up
        ch = (cb + cw[0:1, :] * up_ref[6 + lo:6 + hi, :] + cw[1:2, :] * up_ref[7 + lo:7 + hi, :]
              + cw[2:3, :] * up)
        hid_ref[lo:hi, :] = (ch * (1.0 + jnp.tanh(ch)) * gate).astype(hid_ref.dtype)


def _ffn_in(a, w_ffn_in, conv_w, conv_b, w_ffn_out, *, rows, tf):
    m, k = a.shape
    nf = D_FF // tf
    n_steps = (m // rows) * nf
    d_out = w_ffn_out.shape[1]
    slab = D_FF // n_steps
    assert slab * n_steps == D_FF and slab % 16 == 0
    return pl.pallas_call(
        _ffn_in_kernel,
        out_shape=(jax.ShapeDtypeStruct((m, D_FF), BF16),
                   jax.ShapeDtypeStruct((D_FF, d_out), BF16)),
        grid=(m // rows, nf),
        in_specs=[
            pl.BlockSpec((rows, k), lambda i, j: (i, 0), pipeline_mode=pl.Buffered(1)),
            pl.BlockSpec((k, tf), lambda i, j: (0, j)),
            pl.BlockSpec((k, tf), lambda i, j: (0, nf + j)),
            pl.BlockSpec((CONV_W, tf), lambda i, j: (0, j)),
            pl.BlockSpec((1, tf), lambda i, j: (0, j)),
            pl.BlockSpec((slab, d_out), lambda i, j: (i * nf + j, 0)),
        ],
        out_specs=(pl.BlockSpec((rows, tf), lambda i, j: (i, j)),
                   pl.BlockSpec((slab, d_out), lambda i, j: (i * nf + j, 0))),
        scratch_shapes=[pltpu.VMEM((rows + 8, tf), F32)],
        compiler_params=_params(("parallel", "parallel")),
        name="ffn_in_conv",
    )(a, w_ffn_in, w_ffn_in, conv_w, conv_b.reshape(1, D_FF), w_ffn_out)


def kernel(x, positions, meta_tokens, attn_norm, w_in, w_gate_up, b_gate, ret_norm, gla_norm,
           w_out, ffn_norm, w_ffn_in, conv_w, conv_b, w_ffn_out, final_norm):
    batch, seq, d = x.shape
    rows = N_META + seq
    m = batch * rows

    meta_tokens = meta_tokens.astype(x.dtype)
    pos = jnp.concatenate([
        jnp.broadcast_to(jnp.arange(N_META, dtype=jnp.int32), (batch, N_META)),
        positions.astype(jnp.int32) + N_META], axis=1).reshape(m, 1)
    half = RET_DK // 2
    inv_freq = (ROPE_BASE ** (-jnp.arange(half, dtype=F32) / half)).reshape(1, half)

    w_in_t = jnp.swapaxes(w_in, 1, 2)[0]

    cos, sin = _rope_tables(pos, inv_freq, rows=rows)

    norm_rows = rows // 3
    hn = _embed_norm(x, meta_tokens, attn_norm[0], rows=norm_rows, out_dtype=BF16)
    proj, z = _in_proj(hn, w_in_t, tm=rows, tn=512)

    y_gla = _gla(proj, z, w_gate_up[0].astype(BF16), b_gate[0].reshape(1, -1), gla_norm[0],
                 batch=batch, rows=rows)
    merged = _retention_merge(proj, y_gla, cos, sin, ret_norm[0], batch=batch, rows=rows)

    h1 = _out_proj(merged, w_out[0], x, meta_tokens, tn=256)
    h1n = _rmsnorm(h1, ffn_norm[0], rows=norm_rows, out_dtype=BF16)
    hidden, w_ffn_out_b = _ffn_in(h1n, w_ffn_in[0], conv_w[0], conv_b[0], w_ffn_out[0],
                                  rows=rows, tf=256)
    h2 = _matmul_res(hidden, w_ffn_out_b, h1, tm=norm_rows, tn=256, name="ffn_out")
    return _final_norm(h2, final_norm, batch=batch, rows_per_batch=rows, seq=seq, rows=256)
```

```python
import jax
import jax.numpy as jnp
from jax import lax
from jax.experimental import pallas as pl
from jax.experimental.pallas import tpu as pltpu

F32 = jnp.float32
BF16 = jnp.bfloat16

D_MODEL = 4096
N_META = 16
RET_HEADS, RET_DK, RET_DV = 8, 256, 512
GLA_HEADS, GLA_DK, GLA_DV = 4, 512, 1024
GLA_RANK = 16
GLA_GATE_TAU = 16.0
D_FF = 11008
CONV_W = 3
ROPE_BASE = 10000.0
EPS = 1e-6
RET_QK = RET_HEADS * RET_DK
GLA_QK = GLA_HEADS * GLA_DK
COL_Q_R, COL_K_R = 0, RET_QK
COL_V_R, COL_O_R = 2 * RET_QK, 2 * RET_QK + D_MODEL
COL_Q_G = 2 * RET_QK + 2 * D_MODEL
COL_K_G, COL_V_G = COL_Q_G + GLA_QK, COL_Q_G + 2 * GLA_QK
COL_O_G = COL_V_G + D_MODEL
N_MAIN = COL_O_G + D_MODEL
COL_M_R, COL_M_G = N_MAIN, N_MAIN + D_MODEL
N_PROJ = N_MAIN + 2 * D_MODEL

VMEM_LIMIT_V7X = 56 * 1024 * 1024
LANES = 128
BIG_CHUNK = 256
FFN_ROW_CHUNK = 256
GLA_EXACT_CHUNK = 64
GLA_SUB = 16
GLA_SAFE_SPAN = 60.0


def _params(sem):
    return pltpu.CompilerParams(dimension_semantics=sem, vmem_limit_bytes=VMEM_LIMIT_V7X)


def _iota(shape, dim):
    return lax.broadcasted_iota(jnp.int32, shape, dim)


def _dot(a, b):
    return jnp.dot(a, b, preferred_element_type=F32)


def _dot_nt(a, b):
    return lax.dot_general(a, b, (((1,), (1,)), ((), ())), preferred_element_type=F32)


def _dot_tn(a, b):
    return lax.dot_general(a, b, (((0,), (0,)), ((), ())), preferred_element_type=F32)


def _rmsnorm_kernel(x_ref, g_ref, o_ref):
    x = x_ref[...]
    y = x * lax.rsqrt(jnp.mean(x * x, axis=-1, keepdims=True) + EPS)
    o_ref[...] = (y * g_ref[...]).astype(o_ref.dtype)


def _rmsnorm(x, g, *, rows, out_dtype):
    m, d = x.shape
    return pl.pallas_call(
        _rmsnorm_kernel,
        out_shape=jax.ShapeDtypeStruct((m, d), out_dtype),
        grid=(m // rows,),
        in_specs=[pl.BlockSpec((rows, d), lambda i: (i, 0)),
                  pl.BlockSpec((1, d), lambda i: (0, 0))],
        out_specs=pl.BlockSpec((rows, d), lambda i: (i, 0)),
        compiler_params=_params(("parallel",)),
        name="rmsnorm",
    )(x, g.reshape(1, d))


def _embed_norm_kernel(x_ref, meta_ref, g_ref, o_ref):
    def norm(v):
        y = v * lax.rsqrt(jnp.mean(v * v, axis=-1, keepdims=True) + EPS)
        return (y * g_ref[...]).astype(o_ref.dtype)

    r = pl.program_id(1)
    rows = o_ref.shape[0]

    @pl.when(r == 0)
    def _():
        o_ref[0:N_META, :] = norm(meta_ref[...])
        o_ref[N_META:, :] = norm(x_ref[0, 0:rows - N_META, :])

    @pl.when(r > 0)
    def _():
        o_ref[...] = norm(x_ref[0])


def _embed_norm(x, meta_tokens, g, *, rows, out_dtype):
    batch, seq, d = x.shape
    tiles = (N_META + seq) // rows

    def x_start(r):
        return pl.multiple_of(jnp.maximum(rows * r - N_META, 0), N_META)

    return pl.pallas_call(
        _embed_norm_kernel,
        out_shape=jax.ShapeDtypeStruct((batch * (N_META + seq), d), out_dtype),
        grid=(batch, tiles),
        in_specs=[pl.BlockSpec((pl.Element(1), pl.Element(rows), pl.Element(d)),
                               lambda b, r: (b, x_start(r), 0)),
                  pl.BlockSpec((N_META, d), lambda b, r: (0, 0)),
                  pl.BlockSpec((1, d), lambda b, r: (0, 0))],
        out_specs=pl.BlockSpec((rows, d), lambda b, r: (b * tiles + r, 0)),
        compiler_params=_params(("parallel", "parallel")),
        name="embed_norm",
    )(x, meta_tokens, g.reshape(1, d))


def _final_norm_kernel(x_ref, g_ref, o_ref):
    x = x_ref[...]
    y = x * lax.rsqrt(jnp.mean(x * x, axis=-1, keepdims=True) + EPS)
    o_ref[...] = y * g_ref[...]


def _final_norm(h, g, *, batch, rows_per_batch, seq, rows):
    d = h.shape[1]
    lead = rows_per_batch - seq
    return pl.pallas_call(
        _final_norm_kernel,
        out_shape=jax.ShapeDtypeStruct((batch, seq, d), F32),
        grid=(batch, seq // rows),
        in_specs=[pl.BlockSpec((pl.Element(1), pl.Element(rows), pl.Element(d)),
                               lambda b, j: (b, pl.multiple_of(lead + rows * j, N_META), 0)),
                  pl.BlockSpec((1, d), lambda b, j: (0, 0))],
        out_specs=pl.BlockSpec((1, rows, d), lambda b, j: (b, j, 0)),
        compiler_params=_params(("parallel", "parallel")),
        name="final_norm",
    )(h.reshape(batch, rows_per_batch, d), g.reshape(1, d))


def _mm_res_kernel(a_ref, w_ref, r_ref, o_ref):
    o_ref[...] = r_ref[...] + _dot(a_ref[...], w_ref[...])


def _matmul_res(a, w, residual, *, tm, tn, name):
    m, k = a.shape
    n = w.shape[1]
    return pl.pallas_call(
        _mm_res_kernel,
        out_shape=jax.ShapeDtypeStruct((m, n), F32),
        grid=(m // tm, n // tn),
        in_specs=[pl.BlockSpec((tm, k), lambda i, j: (i, 0)),
                  pl.BlockSpec((k, tn), lambda i, j: (0, j)),
                  pl.BlockSpec((tm, tn), lambda i, j: (i, j))],
        out_specs=pl.BlockSpec((tm, tn), lambda i, j: (i, j)),
        compiler_params=_params(("parallel", "parallel")),
        name=name,
    )(a, w, residual)


def _out_proj_kernel(a_ref, w_ref, x_ref, meta_ref, o_ref):
    w = w_ref[...].astype(BF16)
    seq = x_ref.shape[1]
    acc = _dot(a_ref[0:N_META + BIG_CHUNK, :], w)
    o_ref[0:N_META, :] = meta_ref[...] + acc[0:N_META]
    o_ref[N_META:N_META + BIG_CHUNK, :] = x_ref[0, 0:BIG_CHUNK, :] + acc[N_META:]
    for lo in range(BIG_CHUNK, seq, BIG_CHUNK):
        rows = slice(N_META + lo, N_META + lo + BIG_CHUNK)
        o_ref[rows, :] = x_ref[0, lo:lo + BIG_CHUNK, :] + _dot(a_ref[rows, :], w)


def _out_proj(a, w, x, meta_tokens, *, tn):
    batch, seq, d = x.shape
    m, k = a.shape
    rows = N_META + seq
    return pl.pallas_call(
        _out_proj_kernel,
        out_shape=jax.ShapeDtypeStruct((m, d), F32),
        grid=(batch, d // tn),
        in_specs=[pl.BlockSpec((rows, k), lambda b, j: (b, 0), pipeline_mode=pl.Buffered(1)),
                  pl.BlockSpec((k, tn), lambda b, j: (0, j)),
                  pl.BlockSpec((1, seq, tn), lambda b, j: (b, 0, j)),
                  pl.BlockSpec((N_META, tn), lambda b, j: (0, j))],
        out_specs=pl.BlockSpec((rows, tn), lambda b, j: (b, j)),
        compiler_params=_params(("parallel", "parallel")),
        name="out_proj",
    )(a, w, x, meta_tokens)


def _in_proj_kernel(a_ref, wt_ref, wz_ref, o_ref, z_ref):
    o_ref[...] = _dot_nt(a_ref[...], wt_ref[...].astype(BF16)).astype(o_ref.dtype)

    @pl.when(pl.program_id(1) == 0)
    def _():
        z_ref[...] = _dot_nt(a_ref[...], wz_ref[...].astype(BF16))


def _in_proj(a, w_in_t, *, tm, tn):
    m, k = a.shape

    def row_start(j):
        return pl.multiple_of(j * tn + jnp.where(j * tn >= N_MAIN, GLA_RANK, 0), GLA_RANK)

    return pl.pallas_call(
        _in_proj_kernel,
        out_shape=(jax.ShapeDtypeStruct((m, N_PROJ), BF16), jax.ShapeDtypeStruct((m, GLA_RANK), F32)),
        grid=(m // tm, N_PROJ // tn),
        in_specs=[pl.BlockSpec((tm, k), lambda i, j: (i, 0), pipeline_mode=pl.Buffered(1)),
                  pl.BlockSpec((pl.Element(tn), pl.Element(k)), lambda i, j: (row_start(j), 0)),
                  pl.BlockSpec((GLA_RANK, k), lambda i, j: (N_MAIN // GLA_RANK, 0))],
        out_specs=(pl.BlockSpec((tm, tn), lambda i, j: (i, j)),
                   pl.BlockSpec((tm, GLA_RANK), lambda i, j: (i, 0))),
        compiler_params=_params(("parallel", "arbitrary")),
        name="in_proj",
    )(a, w_in_t, w_in_t)


def _rope_table_kernel(pos_ref, invf_ref, cos_ref, sin_ref):
    ang = pos_ref[...].astype(F32) * invf_ref[...]
    cos_ref[...] = jnp.cos(ang)
    sin_ref[...] = jnp.sin(ang)


def _rope_tables(pos_col, inv_freq, *, rows):
    m = pos_col.shape[0]
    half = inv_freq.shape[1]
    out = jax.ShapeDtypeStruct((m, half), F32)
    return pl.pallas_call(
        _rope_table_kernel,
        out_shape=(out, out),
        grid=(m // rows,),
        in_specs=[pl.BlockSpec((rows, 1), lambda i: (i, 0)),
                  pl.BlockSpec((1, half), lambda i: (0, 0))],
        out_specs=(pl.BlockSpec((rows, half), lambda i: (i, 0)),
                   pl.BlockSpec((rows, half), lambda i: (i, 0))),
        compiler_params=_params(("parallel",)),
        name="rope_tables",
    )(pos_col, inv_freq)


def _lower_tri(n):
    return (_iota((n, n), 0) >= _iota((n, n), 1)).astype(BF16)


def _gla_log_decay(rows, tri, z_ref, wgu_ref, bg_ref):
    u = _dot(z_ref[rows, :].astype(BF16), wgu_ref[...]) + bg_ref[...]
    log_a = (jnp.minimum(u, 0.0) - jnp.log(1.0 + jnp.exp(-jnp.abs(u)))) * (1.0 / GLA_GATE_TAU)
    hi = log_a.astype(BF16)
    r1 = log_a - hi.astype(F32)
    mid = r1.astype(BF16)
    lo = (r1 - mid.astype(F32)).astype(BF16)
    return _dot(tri, hi) + _dot(tri, mid) + _dot(tri, lo)


def _gla_finish(rows, y, q, k, v, b, g_ref, out_ref, s_ref):
    n = b.shape[0]
    b_last = b[n - 1:n, :]
    kd = k * jnp.exp(b_last - b)
    upd = _dot_tn(kd.astype(BF16), v)
    decay_col = jnp.broadcast_to(jnp.exp(b_last), (LANES, GLA_DK)).T
    s_ref[...] = s_ref[...] * jnp.tile(decay_col, (1, GLA_DV // LANES)) + upd
    yn = y * lax.rsqrt(jnp.mean(y * y, axis=-1, keepdims=True) + EPS) * g_ref[...]
    out_ref[rows, :] = yn.astype(out_ref.dtype)


def _gla_chunk_direct(r0, n, b, q_ref, k_ref, v_ref, g_ref, out_ref, s_ref):
    rows = pl.ds(r0, n)
    q = q_ref[rows, :].astype(F32) * (GLA_DK ** -0.5)
    k = k_ref[rows, :].astype(F32)
    v = v_ref[rows, :]
    qe = (q * jnp.exp(b)).astype(BF16)
    cross = _dot(qe, s_ref[...].astype(BF16))
    scores = _dot_nt(qe, (k * jnp.exp(-b)).astype(BF16))
    causal = _iota((n, n), 0) >= _iota((n, n), 1)
    y = _dot(jnp.where(causal, scores, 0.0).astype(BF16), v) + cross
    _gla_finish(rows, y, q, k, v, b, g_ref, out_ref, s_ref)


def _gla_chunk_exact(r0, c_len, q_ref, k_ref, v_ref, z_ref, wgu_ref, bg_ref, g_ref, out_ref, s_ref):
    sub = GLA_SUB
    rows = pl.ds(r0, c_len)
    b = _gla_log_decay(rows, _lower_tri(c_len), z_ref, wgu_ref, bg_ref)
    q = q_ref[rows, :].astype(F32) * (GLA_DK ** -0.5)
    k = k_ref[rows, :].astype(F32)
    v = v_ref[rows, :]
    cross = _dot((q * jnp.exp(b)).astype(BF16), s_ref[...].astype(BF16))

    key_row = _iota((c_len, 1), 0)
    lane = _iota((sub, c_len), 1)
    qrow = _iota((sub, c_len), 0)
    blocks = []
    for s in range(c_len // sub):
        sl = slice(sub * s, sub * (s + 1))
        bs, qs, ks = b[sl], q[sl], k[sl]
        acc = jnp.zeros((sub, c_len), F32)
        for j in range(sub):
            w = jnp.exp(jnp.minimum(bs - bs[j:j + 1, :], 0.0))
            col = jnp.sum(qs * w * ks[j:j + 1, :], axis=-1, keepdims=True)
            acc = jnp.where((lane == sub * s + j) & (qrow >= j), col, acc)
        if s > 0:
            b_ref_row = b[sub * s - 1:sub * s, :]
            qt = qs * jnp.exp(bs - b_ref_row)
            earlier = key_row < sub * s
            kt = jnp.where(earlier, k * jnp.exp(jnp.where(earlier, b_ref_row - b, 0.0)), 0.0)
            acc = acc + _dot_nt(qt.astype(BF16), kt.astype(BF16))
        blocks.append(acc)
    scores = jnp.concatenate(blocks, axis=0)
    y = _dot(scores.astype(BF16), v) + cross
    _gla_finish(rows, y, q, k, v, b, g_ref, out_ref, s_ref)


def _chunk_span(b):
    n = b.shape[0]
    return jnp.max(-b[n - 1:n, :], axis=1, keepdims=True)


def _gla_chunk_any(r0, n, q_ref, k_ref, v_ref, z_ref, wgu_ref, bg_ref, g_ref, out_ref, s_ref, b_ref):
    b = b_ref[pl.ds(r0, n), :]
    small = _chunk_span(b)[0, 0] <= GLA_SAFE_SPAN

    @pl.when(small)
    def _():
        _gla_chunk_direct(r0, n, b, q_ref, k_ref, v_ref, g_ref, out_ref, s_ref)

    @pl.when(jnp.logical_not(small))
    def _():
        c_len = min(n, GLA_EXACT_CHUNK)

        def body(i, carry):
            _gla_chunk_exact(pl.multiple_of(r0 + i * c_len, GLA_SUB), c_len,
                             q_ref, k_ref, v_ref, z_ref, wgu_ref, bg_ref, g_ref, out_ref, s_ref)
            return carry

        lax.fori_loop(0, n // c_len, body, 0)


def _gla_kernel(q_ref, k_ref, v_ref, z_ref, wgu_ref, bg_ref, g_ref, out_ref, s_ref, b_ref):
    refs = (q_ref, k_ref, v_ref, z_ref, wgu_ref, bg_ref, g_ref, out_ref, s_ref)
    n_big = (q_ref.shape[0] - N_META) // BIG_CHUNK

    def big_start(c):
        return pl.multiple_of(N_META + c * BIG_CHUNK, N_META)

    b_meta = _gla_log_decay(pl.ds(0, N_META), _lower_tri(N_META), z_ref, wgu_ref, bg_ref)
    b_ref[0:N_META, :] = b_meta
    tri = _lower_tri(BIG_CHUNK)

    def decay_body(c, span):
        rows = pl.ds(big_start(c), BIG_CHUNK)
        b = _gla_log_decay(rows, tri, z_ref, wgu_ref, bg_ref)
        b_ref[rows, :] = b
        return jnp.maximum(span, _chunk_span(b))

    span = lax.fori_loop(0, n_big, decay_body, _chunk_span(b_meta), unroll=True)
    all_small = span[0, 0] <= GLA_SAFE_SPAN

    s_ref[...] = jnp.zeros_like(s_ref)

    @pl.when(all_small)
    def _():
        direct_refs = (q_ref, k_ref, v_ref, g_ref, out_ref, s_ref)
        _gla_chunk_direct(0, N_META, b_ref[0:N_META, :], *direct_refs)

        def body(c, carry):
            r0 = big_start(c)
            _gla_chunk_direct(r0, BIG_CHUNK, b_ref[pl.ds(r0, BIG_CHUNK), :], *direct_refs)
            return carry

        lax.fori_loop(0, n_big, body, 0)

    @pl.when(jnp.logical_not(all_small))
    def _():
        _gla_chunk_any(0, N_META, *refs, b_ref)

        def body(c, carry):
            _gla_chunk_any(big_start(c), BIG_CHUNK, *refs, b_ref)
            return carry

        lax.fori_loop(0, n_big, body, 0)


def _gla(proj, z, wgu, bg, g_gla, *, batch, rows):
    m = proj.shape[0]
    return pl.pallas_call(
        _gla_kernel,
        out_shape=jax.ShapeDtypeStruct((m, D_MODEL), BF16),
        grid=(batch, GLA_HEADS),
        in_specs=[
            pl.BlockSpec((rows, GLA_DK), lambda b, h: (b, COL_Q_G // GLA_DK + h)),
            pl.BlockSpec((rows, GLA_DK), lambda b, h: (b, COL_K_G // GLA_DK + h)),
            pl.BlockSpec((rows, GLA_DV), lambda b, h: (b, COL_V_G // GLA_DV + h)),
            pl.BlockSpec((rows, GLA_RANK), lambda b, h: (b, 0)),
            pl.BlockSpec((GLA_RANK, GLA_DK), lambda b, h: (0, h)),
            pl.BlockSpec((1, GLA_DK), lambda b, h: (0, h)),
            pl.BlockSpec((1, GLA_DV), lambda b, h: (0, h)),
        ],
        out_specs=pl.BlockSpec((rows, GLA_DV), lambda b, h: (b, h)),
        scratch_shapes=[pltpu.VMEM((GLA_DK, GLA_DV), F32), pltpu.VMEM((rows, GLA_DK), F32)],
        compiler_params=_params(("parallel", "parallel")),
        name="gla",
    )(proj, proj, proj, z, wgu, bg, g_gla.reshape(1, D_MODEL))


def _ret_decays(lg, n):
    rel = (_iota((n, n), 0) - _iota((n, n), 1)).astype(F32)
    d_intra = jnp.where(rel >= 0, jnp.exp(lg * jnp.maximum(rel, 0.0)), 0.0)
    ridx = _iota((n, 1), 0).astype(F32)
    d_q = jnp.exp(lg * (ridx + 1.0))
    d_k = jnp.exp(lg * (n - 1.0 - ridx))
    d_chunk = jnp.exp(lg * float(n))
    return d_intra, d_q, d_k, d_chunk


def _ret_chunk(r0, decays, q_ref, k_ref, v_ref, o_ref, m_ref, cos_ref, sin_ref, g_ref,
               yg_ref, og_ref, mg_ref, out_ref, s_ref):
    d_intra, d_q, d_k, d_chunk = decays
    n = d_intra.shape[0]
    rows = pl.ds(r0, n)
    half = RET_DK // 2
    cos = cos_ref[rows, :]
    sin = sin_ref[rows, :]

    def rope(ref):
        x = ref[rows, :].astype(F32)
        x1, x2 = x[:, :half], x[:, half:]
        return jnp.concatenate([x1 * cos - x2 * sin, x2 * cos + x1 * sin], axis=-1)

    q = rope(q_ref)
    k = rope(k_ref) * (RET_DK ** -0.5)
    v = v_ref[rows, :]

    qb = q.astype(BF16)
    scores = _dot_nt(qb, k.astype(BF16)) * d_intra
    state = s_ref[...]
    y = _dot(scores.astype(BF16), v) + _dot(qb, state.astype(BF16)) * d_q
    s_ref[...] = state * d_chunk + _dot_tn((k * d_k).astype(BF16), v)

    mu = jnp.mean(y, axis=-1, keepdims=True)
    yc = y - mu
    var = jnp.mean(yc * yc, axis=-1, keepdims=True)
    yn = yc * lax.rsqrt(var + EPS) * g_ref[...]

    def gated4(branch, o_gate_ref, m_gate_ref):
        o = o_gate_ref[rows, :]
        m = m_gate_ref[rows, :]
        return (1.0 + jnp.tanh(0.5 * m)) * (1.0 + jnp.tanh(0.5 * o)) * (o * branch)

    merged = 0.25 * (gated4(yn.astype(BF16), o_ref, m_ref) + gated4(yg_ref[rows, :], og_ref, mg_ref))
    out_ref[rows, :] = merged.astype(out_ref.dtype)


def _ret_kernel(lg_ref, q_ref, k_ref, v_ref, o_ref, m_ref, cos_ref, sin_ref, g_ref,
                yg_ref, og_ref, mg_ref, out_ref, s_ref):
    lg = lg_ref[0][:, :1]
    refs = (q_ref, k_ref, v_ref, o_ref, m_ref, cos_ref, sin_ref, g_ref, yg_ref, og_ref, mg_ref,
            out_ref, s_ref)
    s_ref[...] = jnp.zeros_like(s_ref)
    _ret_chunk(0, _ret_decays(lg, N_META), *refs)
    decays = _ret_decays(lg, BIG_CHUNK)

    def body(c, carry):
        r0 = pl.multiple_of(N_META + c * BIG_CHUNK, N_META)
        _ret_chunk(r0, decays, *refs)
        return carry

    lax.fori_loop(0, (q_ref.shape[0] - N_META) // BIG_CHUNK, body, 0)


def _retention_merge(proj, y_gla, cos, sin, g_ret, *, batch, rows):
    m = proj.shape[0]
    log_gamma = jnp.log1p(-jnp.exp2(-5.0 - jnp.arange(RET_HEADS, dtype=F32)))
    lg = jnp.broadcast_to(log_gamma[:, None, None], (RET_HEADS, 1, LANES))

    def cols(start, width):
        return lambda b, h: (b, start // width + h)

    dk, dv = RET_DK, RET_DV
    return pl.pallas_call(
        _ret_kernel,
        out_shape=jax.ShapeDtypeStruct((m, D_MODEL), BF16),
        grid=(batch, RET_HEADS),
        in_specs=[
            pl.BlockSpec((1, 1, LANES), lambda b, h: (h, 0, 0)),
            pl.BlockSpec((rows, dk), cols(COL_Q_R, dk)),
            pl.BlockSpec((rows, dk), cols(COL_K_R, dk)),
            pl.BlockSpec((rows, dv), cols(COL_V_R, dv)),
            pl.BlockSpec((rows, dv), cols(COL_O_R, dv)),
            pl.BlockSpec((rows, dv), cols(COL_M_R, dv)),
            pl.BlockSpec((rows, dk // 2), lambda b, h: (b, 0)),
            pl.BlockSpec((rows, dk // 2), lambda b, h: (b, 0)),
            pl.BlockSpec((1, dv), lambda b, h: (0, h)),
            pl.BlockSpec((rows, dv), lambda b, h: (b, h)),
            pl.BlockSpec((rows, dv), cols(COL_O_G, dv)),
            pl.BlockSpec((rows, dv), cols(COL_M_G, dv)),
        ],
        out_specs=pl.BlockSpec((rows, dv), lambda b, h: (b, h)),
        scratch_shapes=[pltpu.VMEM((dk, dv), F32)],
        compiler_params=_params(("parallel", "parallel")),
        name="retention_merge",
    )(lg, proj, proj, proj, proj, proj, cos, sin, g_ret.reshape(1, D_MODEL), y_gla, proj, proj)


def _ffn_in_kernel(a_ref, wu_ref, wg_ref, cw_ref, cb_ref, wo_ref, hid_ref, wo_bf16_ref):
    wo_bf16_ref[...] = wo_ref[...].astype(BF16)
    rows = a_ref.shape[0]
    tf = wu_ref.shape[1]
    w = jnp.concatenate([wu_ref[...].astype(BF16), wg_ref[...].astype(BF16)], axis=1)
    cw = 0.5 * cw_ref[...]
    cb = 0.5 * cb_ref[...]
    tail = jnp.zeros((8, tf), F32)
    starts = list(range(0, rows - FFN_ROW_CHUNK + 1, FFN_ROW_CHUNK))
    for lo, hi in zip(starts, starts[1:] + [rows]):
        up_gate = _dot(a_ref[lo:hi, :], w)
        up, gate = up_gate[:, :tf], up_gate[:, tf:]
        ext = jnp.concatenate([tail, up], axis=0)
        up_m1 = pltpu.roll(ext, 1, axis=0)[8:]
        up_m2 = pltpu.roll(ext, 2, axis=0)[8:]
        ch = cb + cw[0:1, :] * up_m2 + cw[1:2, :] * up_m1 + cw[2:3, :] * up
        hid_ref[lo:hi, :] = (ch * (1.0 + jnp.tanh(ch)) * gate).astype(hid_ref.dtype)
        tail = up[hi - lo - 8:]


def _ffn_in(a, w_ffn_in, conv_w, conv_b, w_ffn_out, *, rows, tf):
    m, k = a.shape
    nf = D_FF // tf
    n_steps = (m // rows) * nf
    d_out = w_ffn_out.shape[1]
    slab = D_FF // n_steps
    assert slab * n_steps == D_FF and slab % 16 == 0
    return pl.pallas_call(
        _ffn_in_kernel,
        out_shape=(jax.ShapeDtypeStruct((m, D_FF), BF16),
                   jax.ShapeDtypeStruct((D_FF, d_out), BF16)),
        grid=(m // rows, nf),
        in_specs=[
            pl.BlockSpec((rows, k), lambda i, j: (i, 0), pipeline_mode=pl.Buffered(1)),
            pl.BlockSpec((k, tf), lambda i, j: (0, j)),
            pl.BlockSpec((k, tf), lambda i, j: (0, nf + j)),
            pl.BlockSpec((CONV_W, tf), lambda i, j: (0, j)),
            pl.BlockSpec((1, tf), lambda i, j: (0, j)),
            pl.BlockSpec((slab, d_out), lambda i, j: (i * nf + j, 0)),
        ],
        out_specs=(pl.BlockSpec((rows, tf), lambda i, j: (i, j)),
                   pl.BlockSpec((slab, d_out), lambda i, j: (i * nf + j, 0))),
        compiler_params=_params(("parallel", "parallel")),
        name="ffn_in_conv",
    )(a, w_ffn_in, w_ffn_in, conv_w, conv_b.reshape(1, D_FF), w_ffn_out)


def kernel(x, positions, meta_tokens, attn_norm, w_in, w_gate_up, b_gate, ret_norm, gla_norm,
           w_out, ffn_norm, w_ffn_in, conv_w, conv_b, w_ffn_out, final_norm):
    batch, seq, d = x.shape
    rows = N_META + seq
    m = batch * rows

    meta_tokens = meta_tokens.astype(x.dtype)
    pos = jnp.concatenate([
        jnp.broadcast_to(jnp.arange(N_META, dtype=jnp.int32), (batch, N_META)),
        positions.astype(jnp.int32) + N_META], axis=1).reshape(m, 1)
    half = RET_DK // 2
    inv_freq = (ROPE_BASE ** (-jnp.arange(half, dtype=F32) / half)).reshape(1, half)

    w_in_t = jnp.swapaxes(w_in, 1, 2)[0]

    cos, sin = _rope_tables(pos, inv_freq, rows=rows)

    norm_rows = rows // 3
    hn = _embed_norm(x, meta_tokens, attn_norm[0], rows=norm_rows, out_dtype=BF16)
    proj, z = _in_proj(hn, w_in_t, tm=rows, tn=512)

    y_gla = _gla(proj, z, w_gate_up[0].astype(BF16), b_gate[0].reshape(1, -1), gla_norm[0],
                 batch=batch, rows=rows)
    merged = _retention_merge(proj, y_gla, cos, sin, ret_norm[0], batch=batch, rows=rows)

    h1 = _out_proj(merged, w_out[0], x, meta_tokens, tn=256)
    h1n = _rmsnorm(h1, ffn_norm[0], rows=norm_rows, out_dtype=BF16)
    hidden, w_ffn_out_b = _ffn_in(h1n, w_ffn_in[0], conv_w[0], conv_b[0], w_ffn_out[0],
                                  rows=rows, tf=256)
    h2 = _matmul_res(hidden, w_ffn_out_b, h1, tm=norm_rows, tn=256, name="ffn_out")
    return _final_norm(h2, final_norm, batch=batch, rows_per_batch=rows, seq=seq, rows=256)
```

```python
import jax
import jax.numpy as jnp
from jax import lax
from jax.experimental import pallas as pl
from jax.experimental.pallas import tpu as pltpu

F32 = jnp.float32
BF16 = jnp.bfloat16

D_MODEL = 4096
N_META = 16
RET_HEADS, RET_DK, RET_DV = 8, 256, 512
GLA_HEADS, GLA_DK, GLA_DV = 4, 512, 1024
GLA_RANK = 16
GLA_GATE_TAU = 16.0
D_FF = 11008
CONV_W = 3
ROPE_BASE = 10000.0
EPS = 1e-6
RET_QK = RET_HEADS * RET_DK
GLA_QK = GLA_HEADS * GLA_DK
COL_Q_R, COL_K_R = 0, RET_QK
COL_V_R, COL_O_R = 2 * RET_QK, 2 * RET_QK + D_MODEL
COL_Q_G = 2 * RET_QK + 2 * D_MODEL
COL_K_G, COL_V_G = COL_Q_G + GLA_QK, COL_Q_G + 2 * GLA_QK
COL_O_G = COL_V_G + D_MODEL
N_MAIN = COL_O_G + D_MODEL
COL_M_R, COL_M_G = N_MAIN, N_MAIN + D_MODEL
N_PROJ = N_MAIN + 2 * D_MODEL

VMEM_LIMIT_V7X = 56 * 1024 * 1024
LANES = 128
BIG_CHUNK = 256
FFN_ROW_CHUNK = 256
GLA_EXACT_CHUNK = 64
GLA_SUB = 16
GLA_SAFE_SPAN = 60.0


def _params(sem):
    return pltpu.CompilerParams(dimension_semantics=sem, vmem_limit_bytes=VMEM_LIMIT_V7X)


def _iota(shape, dim):
    return lax.broadcasted_iota(jnp.int32, shape, dim)


def _dot(a, b):
    return jnp.dot(a, b, preferred_element_type=F32)


def _dot_nt(a, b):
    return lax.dot_general(a, b, (((1,), (1,)), ((), ())), preferred_element_type=F32)


def _dot_tn(a, b):
    return lax.dot_general(a, b, (((0,), (0,)), ((), ())), preferred_element_type=F32)


def _rmsnorm_kernel(x_ref, g_ref, o_ref):
    x = x_ref[...]
    y = x * lax.rsqrt(jnp.mean(x * x, axis=-1, keepdims=True) + EPS)
    o_ref[...] = (y * g_ref[...]).astype(o_ref.dtype)


def _rmsnorm(x, g, *, rows, out_dtype):
    m, d = x.shape
    return pl.pallas_call(
        _rmsnorm_kernel,
        out_shape=jax.ShapeDtypeStruct((m, d), out_dtype),
        grid=(m // rows,),
        in_specs=[pl.BlockSpec((rows, d), lambda i: (i, 0)),
                  pl.BlockSpec((1, d), lambda i: (0, 0))],
        out_specs=pl.BlockSpec((rows, d), lambda i: (i, 0)),
        compiler_params=_params(("parallel",)),
        name="rmsnorm",
    )(x, g.reshape(1, d))


def _embed_norm_kernel(x_ref, meta_ref, g_ref, o_ref):
    def norm(v):
        y = v * lax.rsqrt(jnp.mean(v * v, axis=-1, keepdims=True) + EPS)
        return (y * g_ref[...]).astype(o_ref.dtype)

    r = pl.program_id(1)
    rows = o_ref.shape[0]

    @pl.when(r == 0)
    def _():
        o_ref[0:N_META, :] = norm(meta_ref[...])
        o_ref[N_META:, :] = norm(x_ref[0, 0:rows - N_META, :])

    @pl.when(r > 0)
    def _():
        o_ref[...] = norm(x_ref[0])


def _embed_norm(x, meta_tokens, g, *, rows, out_dtype):
    batch, seq, d = x.shape
    tiles = (N_META + seq) // rows

    def x_start(r):
        return pl.multiple_of(jnp.maximum(rows * r - N_META, 0), N_META)

    return pl.pallas_call(
        _embed_norm_kernel,
        out_shape=jax.ShapeDtypeStruct((batch * (N_META + seq), d), out_dtype),
        grid=(batch, tiles),
        in_specs=[pl.BlockSpec((pl.Element(1), pl.Element(rows), pl.Element(d)),
                               lambda b, r: (b, x_start(r), 0)),
                  pl.BlockSpec((N_META, d), lambda b, r: (0, 0)),
                  pl.BlockSpec((1, d), lambda b, r: (0, 0))],
        out_specs=pl.BlockSpec((rows, d), lambda b, r: (b * tiles + r, 0)),
        compiler_params=_params(("parallel", "parallel")),
        name="embed_norm",
    )(x, meta_tokens, g.reshape(1, d))


def _final_norm_kernel(x_ref, g_ref, o_ref):
    x = x_ref[...]
    y = x * lax.rsqrt(jnp.mean(x * x, axis=-1, keepdims=True) + EPS)
    o_ref[...] = y * g_ref[...]


def _final_norm(h, g, *, batch, rows_per_batch, seq, rows):
    d = h.shape[1]
    lead = rows_per_batch - seq
    return pl.pallas_call(
        _final_norm_kernel,
        out_shape=jax.ShapeDtypeStruct((batch, seq, d), F32),
        grid=(batch, seq // rows),
        in_specs=[pl.BlockSpec((pl.Element(1), pl.Element(rows), pl.Element(d)),
                               lambda b, j: (b, pl.multiple_of(lead + rows * j, N_META), 0)),
                  pl.BlockSpec((1, d), lambda b, j: (0, 0))],
        out_specs=pl.BlockSpec((1, rows, d), lambda b, j: (b, j, 0)),
        compiler_params=_params(("parallel", "parallel")),
        name="final_norm",
    )(h.reshape(batch, rows_per_batch, d), g.reshape(1, d))


def _mm_res_kernel(a_ref, w_ref, r_ref, o_ref):
    o_ref[...] = r_ref[...] + _dot(a_ref[...], w_ref[...])


def _matmul_res(a, w, residual, *, tm, tn, name):
    m, k = a.shape
    n = w.shape[1]
    return pl.pallas_call(
        _mm_res_kernel,
        out_shape=jax.ShapeDtypeStruct((m, n), F32),
        grid=(m // tm, n // tn),
        in_specs=[pl.BlockSpec((tm, k), lambda i, j: (i, 0)),
                  pl.BlockSpec((k, tn), lambda i, j: (0, j)),
                  pl.BlockSpec((tm, tn), lambda i, j: (i, j))],
        out_specs=pl.BlockSpec((tm, tn), lambda i, j: (i, j)),
        compiler_params=_params(("parallel", "parallel")),
        name=name,
    )(a, w, residual)


def _out_proj_kernel(a_ref, w_ref, x_ref, meta_ref, o_ref):
    w = w_ref[...].astype(BF16)
    seq = x_ref.shape[1]
    acc = _dot(a_ref[0:N_META + BIG_CHUNK, :], w)
    o_ref[0:N_META, :] = meta_ref[pl.program_id(1)] + acc[0:N_META]
    o_ref[N_META:N_META + BIG_CHUNK, :] = x_ref[0, 0:BIG_CHUNK, :] + acc[N_META:]
    for lo in range(BIG_CHUNK, seq, BIG_CHUNK):
        rows = slice(N_META + lo, N_META + lo + BIG_CHUNK)
        o_ref[rows, :] = x_ref[0, lo:lo + BIG_CHUNK, :] + _dot(a_ref[rows, :], w)


def _out_proj(a, w, x, meta_tokens, *, tn):
    batch, seq, d = x.shape
    m, k = a.shape
    rows = N_META + seq
    nt = d // tn
    meta_tiles = meta_tokens.reshape(N_META, nt, tn).transpose(1, 0, 2)
    return pl.pallas_call(
        _out_proj_kernel,
        out_shape=jax.ShapeDtypeStruct((m, d), F32),
        grid=(batch, nt),
        in_specs=[pl.BlockSpec((rows, k), lambda b, j: (b, 0), pipeline_mode=pl.Buffered(1)),
                  pl.BlockSpec((k, tn), lambda b, j: (0, j)),
                  pl.BlockSpec((1, seq, tn), lambda b, j: (b, 0, j)),
                  pl.BlockSpec((nt, N_META, tn), lambda b, j: (0, 0, 0))],
        out_specs=pl.BlockSpec((rows, tn), lambda b, j: (b, j)),
        compiler_params=_params(("parallel", "parallel")),
        name="out_proj",
    )(a, w, x, meta_tiles)


def _in_proj_kernel(a_ref, wt_ref, wz_ref, o_ref, z_ref):
    o_ref[...] = _dot_nt(a_ref[...], wt_ref[...].astype(BF16)).astype(o_ref.dtype)

    @pl.when(pl.program_id(1) == 0)
    def _():
        z_ref[...] = _dot_nt(a_ref[...], wz_ref[...].astype(BF16))


def _in_proj(a, w_in_t, *, tm, tn):
    m, k = a.shape

    def row_start(j):
        return pl.multiple_of(j * tn + jnp.where(j * tn >= N_MAIN, GLA_RANK, 0), GLA_RANK)

    return pl.pallas_call(
        _in_proj_kernel,
        out_shape=(jax.ShapeDtypeStruct((m, N_PROJ), BF16), jax.ShapeDtypeStruct((m, GLA_RANK), F32)),
        grid=(m // tm, N_PROJ // tn),
        in_specs=[pl.BlockSpec((tm, k), lambda i, j: (i, 0), pipeline_mode=pl.Buffered(1)),
                  pl.BlockSpec((pl.Element(tn), pl.Element(k)), lambda i, j: (row_start(j), 0)),
                  pl.BlockSpec((GLA_RANK, k), lambda i, j: (N_MAIN // GLA_RANK, 0))],
        out_specs=(pl.BlockSpec((tm, tn), lambda i, j: (i, j)),
                   pl.BlockSpec((tm, GLA_RANK), lambda i, j: (i, 0))),
        compiler_params=_params(("parallel", "arbitrary")),
        name="in_proj",
    )(a, w_in_t, w_in_t)


def _rope_table_kernel(pos_ref, invf_ref, cos_ref, sin_ref):
    ang = pos_ref[...].astype(F32) * invf_ref[...]
    cos_ref[...] = jnp.cos(ang)
    sin_ref[...] = jnp.sin(ang)


def _rope_tables(pos_col, inv_freq, *, rows):
    m = pos_col.shape[0]
    half = inv_freq.shape[1]
    out = jax.ShapeDtypeStruct((m, half), F32)
    return pl.pallas_call(
        _rope_table_kernel,
        out_shape=(out, out),
        grid=(m // rows,),
        in_specs=[pl.BlockSpec((rows, 1), lambda i: (i, 0)),
                  pl.BlockSpec((1, half), lambda i: (0, 0))],
        out_specs=(pl.BlockSpec((rows, half), lambda i: (i, 0)),
                   pl.BlockSpec((rows, half), lambda i: (i, 0))),
        compiler_params=_params(("parallel",)),
        name="rope_tables",
    )(pos_col, inv_freq)


def _lower_tri(n):
    return (_iota((n, n), 0) >= _iota((n, n), 1)).astype(BF16)


def _gla_log_decay(rows, tri, z_ref, wgu_ref, bg_ref):
    u = _dot(z_ref[rows, :].astype(BF16), wgu_ref[...]) + bg_ref[...]
    log_a = (jnp.minimum(u, 0.0) - jnp.log(1.0 + jnp.exp(-jnp.abs(u)))) * (1.0 / GLA_GATE_TAU)
    hi = log_a.astype(BF16)
    r1 = log_a - hi.astype(F32)
    mid = r1.astype(BF16)
    lo = (r1 - mid.astype(F32)).astype(BF16)
    return _dot(tri, hi) + _dot(tri, mid) + _dot(tri, lo)


def _gla_finish(rows, y, q, k, v, b, g_ref, out_ref, s_ref):
    n = b.shape[0]
    b_last = b[n - 1:n, :]
    kd = k * jnp.exp(b_last - b)
    upd = _dot_tn(kd.astype(BF16), v)
    decay_col = jnp.broadcast_to(jnp.exp(b_last), (LANES, GLA_DK)).T
    s_ref[...] = s_ref[...] * jnp.tile(decay_col, (1, GLA_DV // LANES)) + upd
    yn = y * lax.rsqrt(jnp.mean(y * y, axis=-1, keepdims=True) + EPS) * g_ref[...]
    out_ref[rows, :] = yn.astype(out_ref.dtype)


def _gla_chunk_direct(r0, n, b, q_ref, k_ref, v_ref, g_ref, out_ref, s_ref):
    rows = pl.ds(r0, n)
    q = q_ref[rows, :].astype(F32) * (GLA_DK ** -0.5)
    k = k_ref[rows, :].astype(F32)
    v = v_ref[rows, :]
    qe = (q * jnp.exp(b)).astype(BF16)
    cross = _dot(qe, s_ref[...].astype(BF16))
    scores = _dot_nt(qe, (k * jnp.exp(-b)).astype(BF16))
    causal = _iota((n, n), 0) >= _iota((n, n), 1)
    y = _dot(jnp.where(causal, scores, 0.0).astype(BF16), v) + cross
    _gla_finish(rows, y, q, k, v, b, g_ref, out_ref, s_ref)


def _gla_chunk_exact(r0, c_len, q_ref, k_ref, v_ref, z_ref, wgu_ref, bg_ref, g_ref, out_ref, s_ref):
    sub = GLA_SUB
    rows = pl.ds(r0, c_len)
    b = _gla_log_decay(rows, _lower_tri(c_len), z_ref, wgu_ref, bg_ref)
    q = q_ref[rows, :].astype(F32) * (GLA_DK ** -0.5)
    k = k_ref[rows, :].astype(F32)
    v = v_ref[rows, :]
    cross = _dot((q * jnp.exp(b)).astype(BF16), s_ref[...].astype(BF16))

    key_row = _iota((c_len, 1), 0)
    lane = _iota((sub, c_len), 1)
    qrow = _iota((sub, c_len), 0)
    blocks = []
    for s in range(c_len // sub):
        sl = slice(sub * s, sub * (s + 1))
        bs, qs, ks = b[sl], q[sl], k[sl]
        acc = jnp.zeros((sub, c_len), F32)
        for j in range(sub):
            w = jnp.exp(jnp.minimum(bs - bs[j:j + 1, :], 0.0))
            col = jnp.sum(qs * w * ks[j:j + 1, :], axis=-1, keepdims=True)
            acc = jnp.where((lane == sub * s + j) & (qrow >= j), col, acc)
        if s > 0:
            b_ref_row = b[sub * s - 1:sub * s, :]
            qt = qs * jnp.exp(bs - b_ref_row)
            earlier = key_row < sub * s
            kt = jnp.where(earlier, k * jnp.exp(jnp.where(earlier, b_ref_row - b, 0.0)), 0.0)
            acc = acc + _dot_nt(qt.astype(BF16), kt.astype(BF16))
        blocks.append(acc)
    scores = jnp.concatenate(blocks, axis=0)
    y = _dot(scores.astype(BF16), v) + cross
    _gla_finish(rows, y, q, k, v, b, g_ref, out_ref, s_ref)


def _chunk_span(b):
    n = b.shape[0]
    return jnp.max(-b[n - 1:n, :], axis=1, keepdims=True)


def _gla_chunk_any(r0, n, q_ref, k_ref, v_ref, z_ref, wgu_ref, bg_ref, g_ref, out_ref, s_ref, b_ref):
    b = b_ref[pl.ds(r0, n), :]
    small = _chunk_span(b)[0, 0] <= GLA_SAFE_SPAN

    @pl.when(small)
    def _():
        _gla_chunk_direct(r0, n, b, q_ref, k_ref, v_ref, g_ref, out_ref, s_ref)

    @pl.when(jnp.logical_not(small))
    def _():
        c_len = min(n, GLA_EXACT_CHUNK)

        def body(i, carry):
            _gla_chunk_exact(pl.multiple_of(r0 + i * c_len, GLA_SUB), c_len,
                             q_ref, k_ref, v_ref, z_ref, wgu_ref, bg_ref, g_ref, out_ref, s_ref)
            return carry

        lax.fori_loop(0, n // c_len, body, 0)


def _gla_kernel(q_ref, k_ref, v_ref, z_ref, wgu_ref, bg_ref, g_ref, out_ref, s_ref, b_ref):
    refs = (q_ref, k_ref, v_ref, z_ref, wgu_ref, bg_ref, g_ref, out_ref, s_ref)
    n_big = (q_ref.shape[0] - N_META) // BIG_CHUNK

    def big_start(c):
        return pl.multiple_of(N_META + c * BIG_CHUNK, N_META)

    b_meta = _gla_log_decay(pl.ds(0, N_META), _lower_tri(N_META), z_ref, wgu_ref, bg_ref)
    b_ref[0:N_META, :] = b_meta
    tri = _lower_tri(BIG_CHUNK)

    def decay_body(c, span):
        rows = pl.ds(big_start(c), BIG_CHUNK)
        b = _gla_log_decay(rows, tri, z_ref, wgu_ref, bg_ref)
        b_ref[rows, :] = b
        return jnp.maximum(span, _chunk_span(b))

    span = lax.fori_loop(0, n_big, decay_body, _chunk_span(b_meta), unroll=True)
    all_small = span[0, 0] <= GLA_SAFE_SPAN

    s_ref[...] = jnp.zeros_like(s_ref)

    @pl.when(all_small)
    def _():
        direct_refs = (q_ref, k_ref, v_ref, g_ref, out_ref, s_ref)
        _gla_chunk_direct(0, N_META, b_ref[0:N_META, :], *direct_refs)

        def body(c, carry):
            r0 = big_start(c)
            _gla_chunk_direct(r0, BIG_CHUNK, b_ref[pl.ds(r0, BIG_CHUNK), :], *direct_refs)
            return carry

        lax.fori_loop(0, n_big, body, 0)

    @pl.when(jnp.logical_not(all_small))
    def _():
        _gla_chunk_any(0, N_META, *refs, b_ref)

        def body(c, carry):
            _gla_chunk_any(big_start(c), BIG_CHUNK, *refs, b_ref)
            return carry

        lax.fori_loop(0, n_big, body, 0)


def _gla(proj, z, wgu, bg, g_gla, *, batch, rows):
    m = proj.shape[0]
    return pl.pallas_call(
        _gla_kernel,
        out_shape=jax.ShapeDtypeStruct((m, D_MODEL), BF16),
        grid=(batch, GLA_HEADS),
        in_specs=[
            pl.BlockSpec((rows, GLA_DK), lambda b, h: (b, COL_Q_G // GLA_DK + h)),
            pl.BlockSpec((rows, GLA_DK), lambda b, h: (b, COL_K_G // GLA_DK + h)),
            pl.BlockSpec((rows, GLA_DV), lambda b, h: (b, COL_V_G // GLA_DV + h)),
            pl.BlockSpec((rows, GLA_RANK), lambda b, h: (b, 0)),
            pl.BlockSpec((GLA_RANK, GLA_DK), lambda b, h: (0, h)),
            pl.BlockSpec((1, GLA_DK), lambda b, h: (0, h)),
            pl.BlockSpec((1, GLA_DV), lambda b, h: (0, h)),
        ],
        out_specs=pl.BlockSpec((rows, GLA_DV), lambda b, h: (b, h)),
        scratch_shapes=[pltpu.VMEM((GLA_DK, GLA_DV), F32), pltpu.VMEM((rows, GLA_DK), F32)],
        compiler_params=_params(("parallel", "parallel")),
        name="gla",
    )(proj, proj, proj, z, wgu, bg, g_gla.reshape(1, D_MODEL))


def _ret_decays(lg, n):
    rel = (_iota((n, n), 0) - _iota((n, n), 1)).astype(F32)
    d_intra = jnp.where(rel >= 0, jnp.exp(lg * jnp.maximum(rel, 0.0)), 0.0)
    ridx = _iota((n, 1), 0).astype(F32)
    d_q = jnp.exp(lg * (ridx + 1.0))
    d_k = jnp.exp(lg * (n - 1.0 - ridx))
    d_chunk = jnp.exp(lg * float(n))
    return d_intra, d_q, d_k, d_chunk


def _ret_chunk(r0, decays, q_ref, k_ref, v_ref, o_ref, m_ref, cos_ref, sin_ref, g_ref,
               yg_ref, og_ref, mg_ref, out_ref, s_ref):
    d_intra, d_q, d_k, d_chunk = decays
    n = d_intra.shape[0]
    rows = pl.ds(r0, n)
    half = RET_DK // 2
    cos = cos_ref[rows, :]
    sin = sin_ref[rows, :]

    def rope(ref):
        x = ref[rows, :].astype(F32)
        x1, x2 = x[:, :half], x[:, half:]
        return jnp.concatenate([x1 * cos - x2 * sin, x2 * cos + x1 * sin], axis=-1)

    q = rope(q_ref)
    k = rope(k_ref) * (RET_DK ** -0.5)
    v = v_ref[rows, :]

    qb = q.astype(BF16)
    scores = _dot_nt(qb, k.astype(BF16)) * d_intra
    state = s_ref[...]
    y = _dot(scores.astype(BF16), v) + _dot(qb, state.astype(BF16)) * d_q
    s_ref[...] = state * d_chunk + _dot_tn((k * d_k).astype(BF16), v)

    mu = jnp.mean(y, axis=-1, keepdims=True)
    yc = y - mu
    var = jnp.mean(yc * yc, axis=-1, keepdims=True)
    yn = yc * lax.rsqrt(var + EPS) * g_ref[...]

    def gated4(branch, o_gate_ref, m_gate_ref):
        o = o_gate_ref[rows, :]
        m = m_gate_ref[rows, :]
        return (1.0 + jnp.tanh(0.5 * m)) * (1.0 + jnp.tanh(0.5 * o)) * (o * branch)

    merged = 0.25 * (gated4(yn.astype(BF16), o_ref, m_ref) + gated4(yg_ref[rows, :], og_ref, mg_ref))
    out_ref[rows, :] = merged.astype(out_ref.dtype)


def _ret_kernel(lg_ref, q_ref, k_ref, v_ref, o_ref, m_ref, cos_ref, sin_ref, g_ref,
                yg_ref, og_ref, mg_ref, out_ref, s_ref):
    lg = lg_ref[0][:, :1]
    refs = (q_ref, k_ref, v_ref, o_ref, m_ref, cos_ref, sin_ref, g_ref, yg_ref, og_ref, mg_ref,
            out_ref, s_ref)
    s_ref[...] = jnp.zeros_like(s_ref)
    _ret_chunk(0, _ret_decays(lg, N_META), *refs)
    decays = _ret_decays(lg, BIG_CHUNK)

    def body(c, carry):
        r0 = pl.multiple_of(N_META + c * BIG_CHUNK, N_META)
        _ret_chunk(r0, decays, *refs)
        return carry

    lax.fori_loop(0, (q_ref.shape[0] - N_META) // BIG_CHUNK, body, 0)


def _retention_merge(proj, y_gla, cos, sin, g_ret, *, batch, rows):
    m = proj.shape[0]
    log_gamma = jnp.log1p(-jnp.exp2(-5.0 - jnp.arange(RET_HEADS, dtype=F32)))
    lg = jnp.broadcast_to(log_gamma[:, None, None], (RET_HEADS, 1, LANES))

    def cols(start, width):
        return lambda b, h: (b, start // width + h)

    dk, dv = RET_DK, RET_DV
    return pl.pallas_call(
        _ret_kernel,
        out_shape=jax.ShapeDtypeStruct((m, D_MODEL), BF16),
        grid=(batch, RET_HEADS),
        in_specs=[
            pl.BlockSpec((1, 1, LANES), lambda b, h: (h, 0, 0)),
            pl.BlockSpec((rows, dk), cols(COL_Q_R, dk)),
            pl.BlockSpec((rows, dk), cols(COL_K_R, dk)),
            pl.BlockSpec((rows, dv), cols(COL_V_R, dv)),
            pl.BlockSpec((rows, dv), cols(COL_O_R, dv)),
            pl.BlockSpec((rows, dv), cols(COL_M_R, dv)),
            pl.BlockSpec((rows, dk // 2), lambda b, h: (b, 0)),
            pl.BlockSpec((rows, dk // 2), lambda b, h: (b, 0)),
            pl.BlockSpec((1, dv), lambda b, h: (0, h)),
            pl.BlockSpec((rows, dv), lambda b, h: (b, h)),
            pl.BlockSpec((rows, dv), cols(COL_O_G, dv)),
            pl.BlockSpec((rows, dv), cols(COL_M_G, dv)),
        ],
        out_specs=pl.BlockSpec((rows, dv), lambda b, h: (b, h)),
        scratch_shapes=[pltpu.VMEM((dk, dv), F32)],
        compiler_params=_params(("parallel", "parallel")),
        name="retention_merge",
    )(lg, proj, proj, proj, proj, proj, cos, sin, g_ret.reshape(1, D_MODEL), y_gla, proj, proj)


def _ffn_in_kernel(a_ref, wu_ref, wg_ref, conv_ref, wo_ref, hid_ref, wo_bf16_ref):
    wo_bf16_ref[...] = wo_ref[...].astype(BF16)
    rows = a_ref.shape[0]
    tf = wu_ref.shape[1]
    w = jnp.concatenate([wu_ref[...].astype(BF16), wg_ref[...].astype(BF16)], axis=1)
    conv = 0.5 * conv_ref[pl.program_id(1)]
    cw, cb = conv[0:CONV_W, :], conv[CONV_W:CONV_W + 1, :]
    tail = jnp.zeros((8, tf), F32)
    starts = list(range(0, rows - FFN_ROW_CHUNK + 1, FFN_ROW_CHUNK))
    for lo, hi in zip(starts, starts[1:] + [rows]):
        up_gate = _dot(a_ref[lo:hi, :], w)
        up, gate = up_gate[:, :tf], up_gate[:, tf:]
        ext = jnp.concatenate([tail, up], axis=0)
        up_m1 = pltpu.roll(ext, 1, axis=0)[8:]
        up_m2 = pltpu.roll(ext, 2, axis=0)[8:]
        ch = cb + cw[0:1, :] * up_m2 + cw[1:2, :] * up_m1 + cw[2:3, :] * up
        hid_ref[lo:hi, :] = (ch * (1.0 + jnp.tanh(ch)) * gate).astype(hid_ref.dtype)
        tail = up[hi - lo - 8:]


def _ffn_in(a, w_ffn_in, conv_w, conv_b, w_ffn_out, *, rows, tf):
    m, k = a.shape
    nf = D_FF // tf
    n_steps = (m // rows) * nf
    d_out = w_ffn_out.shape[1]
    slab = D_FF // n_steps
    assert slab * n_steps == D_FF and slab % 16 == 0
    conv = jnp.concatenate([conv_w, conv_b.reshape(1, D_FF)], axis=0)
    conv_tiles = conv.reshape(CONV_W + 1, nf, tf).transpose(1, 0, 2)
    return pl.pallas_call(
        _ffn_in_kernel,
        out_shape=(jax.ShapeDtypeStruct((m, D_FF), BF16),
                   jax.ShapeDtypeStruct((D_FF, d_out), BF16)),
        grid=(m // rows, nf),
        in_specs=[
            pl.BlockSpec((rows, k), lambda i, j: (i, 0), pipeline_mode=pl.Buffered(1)),
            pl.BlockSpec((k, tf), lambda i, j: (0, j)),
            pl.BlockSpec((k, tf), lambda i, j: (0, nf + j)),
            pl.BlockSpec((nf, CONV_W + 1, tf), lambda i, j: (0, 0, 0)),
            pl.BlockSpec((slab, d_out), lambda i, j: (i * nf + j, 0)),
        ],
        out_specs=(pl.BlockSpec((rows, tf), lambda i, j: (i, j)),
                   pl.BlockSpec((slab, d_out), lambda i, j: (i * nf + j, 0))),
        compiler_params=_params(("parallel", "parallel")),
        name="ffn_in_conv",
    )(a, w_ffn_in, w_ffn_in, conv_tiles, w_ffn_out)


def kernel(x, positions, meta_tokens, attn_norm, w_in, w_gate_up, b_gate, ret_norm, gla_norm,
           w_out, ffn_norm, w_ffn_in, conv_w, conv_b, w_ffn_out, final_norm):
    batch, seq, d = x.shape
    rows = N_META + seq
    m = batch * rows

    meta_tokens = meta_tokens.astype(x.dtype)
    pos = jnp.concatenate([
        jnp.broadcast_to(jnp.arange(N_META, dtype=jnp.int32), (batch, N_META)),
        positions.astype(jnp.int32) + N_META], axis=1).reshape(m, 1)
    half = RET_DK // 2
    inv_freq = (ROPE_BASE ** (-jnp.arange(half, dtype=F32) / half)).reshape(1, half)

    w_in_t = jnp.swapaxes(w_in, 1, 2)[0]

    cos, sin = _rope_tables(pos, inv_freq, rows=rows)

    norm_rows = rows // 3
    hn = _embed_norm(x, meta_tokens, attn_norm[0], rows=norm_rows, out_dtype=BF16)
    proj, z = _in_proj(hn, w_in_t, tm=rows, tn=512)

    y_gla = _gla(proj, z, w_gate_up[0].astype(BF16), b_gate[0].reshape(1, -1), gla_norm[0],
                 batch=batch, rows=rows)
    merged = _retention_merge(proj, y_gla, cos, sin, ret_norm[0], batch=batch, rows=rows)

    h1 = _out_proj(merged, w_out[0], x, meta_tokens, tn=256)
    h1n = _rmsnorm(h1, ffn_norm[0], rows=norm_rows, out_dtype=BF16)
    hidden, w_ffn_out_b = _ffn_in(h1n, w_ffn_in[0], conv_w[0], conv_b[0], w_ffn_out[0],
                                  rows=rows, tf=256)
    h2 = _matmul_res(hidden, w_ffn_out_b, h1, tm=norm_rows, tn=256, name="ffn_out")
    return _final_norm(h2, final_norm, batch=batch, rows_per_batch=rows, seq=seq, rows=256)
```

```python
import jax
import jax.numpy as jnp
from jax import lax
from jax.experimental import pallas as pl
from jax.experimental.pallas import tpu as pltpu

F32 = jnp.float32
BF16 = jnp.bfloat16

D_MODEL = 4096
N_META = 16
RET_HEADS, RET_DK, RET_DV = 8, 256, 512
GLA_HEADS, GLA_DK, GLA_DV = 4, 512, 1024
GLA_RANK = 16
GLA_GATE_TAU = 16.0
D_FF = 11008
CONV_W = 3
ROPE_BASE = 10000.0
EPS = 1e-6
RET_QK = RET_HEADS * RET_DK
GLA_QK = GLA_HEADS * GLA_DK
COL_Q_R, COL_K_R = 0, RET_QK
COL_V_R, COL_O_R = 2 * RET_QK, 2 * RET_QK + D_MODEL
COL_Q_G = 2 * RET_QK + 2 * D_MODEL
COL_K_G, COL_V_G = COL_Q_G + GLA_QK, COL_Q_G + 2 * GLA_QK
COL_O_G = COL_V_G + D_MODEL
N_MAIN = COL_O_G + D_MODEL
COL_M_R, COL_M_G = N_MAIN, N_MAIN + D_MODEL
N_PROJ = N_MAIN + 2 * D_MODEL

VMEM_LIMIT_V7X = 56 * 1024 * 1024
LANES = 128
BIG_CHUNK = 256
FFN_ROW_CHUNK = 256
GLA_EXACT_CHUNK = 64
GLA_SUB = 16
GLA_SAFE_SPAN = 60.0


def _params(sem):
    return pltpu.CompilerParams(dimension_semantics=sem, vmem_limit_bytes=VMEM_LIMIT_V7X)


def _iota(shape, dim):
    return lax.broadcasted_iota(jnp.int32, shape, dim)


def _dot(a, b):
    return jnp.dot(a, b, preferred_element_type=F32)


def _dot_nt(a, b):
    return lax.dot_general(a, b, (((1,), (1,)), ((), ())), preferred_element_type=F32)


def _dot_tn(a, b):
    return lax.dot_general(a, b, (((0,), (0,)), ((), ())), preferred_element_type=F32)


def _rmsnorm_kernel(x_ref, g_ref, o_ref):
    x = x_ref[...]
    y = x * lax.rsqrt(jnp.mean(x * x, axis=-1, keepdims=True) + EPS)
    o_ref[...] = (y * g_ref[...]).astype(o_ref.dtype)


def _rmsnorm(x, g, *, rows, out_dtype):
    m, d = x.shape
    return pl.pallas_call(
        _rmsnorm_kernel,
        out_shape=jax.ShapeDtypeStruct((m, d), out_dtype),
        grid=(m // rows,),
        in_specs=[pl.BlockSpec((rows, d), lambda i: (i, 0)),
                  pl.BlockSpec((1, d), lambda i: (0, 0))],
        out_specs=pl.BlockSpec((rows, d), lambda i: (i, 0)),
        compiler_params=_params(("parallel",)),
        name="rmsnorm",
    )(x, g.reshape(1, d))


def _embed_norm_kernel(x_ref, meta_ref, g_ref, o_ref):
    def norm(v):
        y = v * lax.rsqrt(jnp.mean(v * v, axis=-1, keepdims=True) + EPS)
        return (y * g_ref[...]).astype(o_ref.dtype)

    r = pl.program_id(1)
    rows = o_ref.shape[0]

    @pl.when(r == 0)
    def _():
        o_ref[0:N_META, :] = norm(meta_ref[...])
        o_ref[N_META:, :] = norm(x_ref[0, 0:rows - N_META, :])

    @pl.when(r > 0)
    def _():
        o_ref[...] = norm(x_ref[0])


def _embed_norm(x, meta_tokens, g, *, rows, out_dtype):
    batch, seq, d = x.shape
    tiles = (N_META + seq) // rows

    def x_start(r):
        return pl.multiple_of(jnp.maximum(rows * r - N_META, 0), N_META)

    return pl.pallas_call(
        _embed_norm_kernel,
        out_shape=jax.ShapeDtypeStruct((batch * (N_META + seq), d), out_dtype),
        grid=(batch, tiles),
        in_specs=[pl.BlockSpec((pl.Element(1), pl.Element(rows), pl.Element(d)),
                               lambda b, r: (b, x_start(r), 0)),
                  pl.BlockSpec((N_META, d), lambda b, r: (0, 0)),
                  pl.BlockSpec((1, d), lambda b, r: (0, 0))],
        out_specs=pl.BlockSpec((rows, d), lambda b, r: (b * tiles + r, 0)),
        compiler_params=_params(("parallel", "parallel")),
        name="embed_norm",
    )(x, meta_tokens, g.reshape(1, d))


def _final_norm_kernel(x_ref, g_ref, o_ref):
    x = x_ref[...]
    y = x * lax.rsqrt(jnp.mean(x * x, axis=-1, keepdims=True) + EPS)
    o_ref[...] = y * g_ref[...]


def _final_norm(h, g, *, batch, rows_per_batch, seq, rows):
    d = h.shape[1]
    lead = rows_per_batch - seq
    return pl.pallas_call(
        _final_norm_kernel,
        out_shape=jax.ShapeDtypeStruct((batch, seq, d), F32),
        grid=(batch, seq // rows),
        in_specs=[pl.BlockSpec((pl.Element(1), pl.Element(rows), pl.Element(d)),
                               lambda b, j: (b, pl.multiple_of(lead + rows * j, N_META), 0)),
                  pl.BlockSpec((1, d), lambda b, j: (0, 0))],
        out_specs=pl.BlockSpec((1, rows, d), lambda b, j: (b, j, 0)),
        compiler_params=_params(("parallel", "parallel")),
        name="final_norm",
    )(h.reshape(batch, rows_per_batch, d), g.reshape(1, d))


def _mm_res_kernel(a_ref, w_ref, r_ref, o_ref):
    o_ref[...] = r_ref[...] + _dot(a_ref[...], w_ref[0])


def _matmul_res(a, w_tiles, residual, *, tm, name):
    m, k = a.shape
    nt, _, tn = w_tiles.shape
    n = nt * tn
    return pl.pallas_call(
        _mm_res_kernel,
        out_shape=jax.ShapeDtypeStruct((m, n), F32),
        grid=(m // tm, nt),
        in_specs=[pl.BlockSpec((tm, k), lambda i, j: (i, 0)),
                  pl.BlockSpec((1, k, tn), lambda i, j: (j, 0, 0)),
                  pl.BlockSpec((tm, tn), lambda i, j: (i, j))],
        out_specs=pl.BlockSpec((tm, tn), lambda i, j: (i, j)),
        compiler_params=_params(("parallel", "parallel")),
        name=name,
    )(a, w_tiles, residual)


def _out_proj_kernel(a_ref, w_ref, x_ref, meta_ref, o_ref):
    w = w_ref[...].astype(BF16)
    seq = x_ref.shape[1]
    acc = _dot(a_ref[0:N_META + BIG_CHUNK, :], w)
    o_ref[0:N_META, :] = meta_ref[...] + acc[0:N_META]
    o_ref[N_META:N_META + BIG_CHUNK, :] = x_ref[0, 0:BIG_CHUNK, :] + acc[N_META:]
    for lo in range(BIG_CHUNK, seq, BIG_CHUNK):
        rows = slice(N_META + lo, N_META + lo + BIG_CHUNK)
        o_ref[rows, :] = x_ref[0, lo:lo + BIG_CHUNK, :] + _dot(a_ref[rows, :], w)


def _out_proj(a, w, x, meta_tokens, *, tn):
    batch, seq, d = x.shape
    m, k = a.shape
    rows = N_META + seq
    return pl.pallas_call(
        _out_proj_kernel,
        out_shape=jax.ShapeDtypeStruct((m, d), F32),
        grid=(batch, d // tn),
        in_specs=[pl.BlockSpec((rows, k), lambda b, j: (b, 0), pipeline_mode=pl.Buffered(1)),
                  pl.BlockSpec((k, tn), lambda b, j: (0, j)),
                  pl.BlockSpec((1, seq, tn), lambda b, j: (b, 0, j)),
                  pl.BlockSpec((N_META, tn), lambda b, j: (0, j))],
        out_specs=pl.BlockSpec((rows, tn), lambda b, j: (b, j)),
        compiler_params=_params(("parallel", "parallel")),
        name="out_proj",
    )(a, w, x, meta_tokens)


def _in_proj_kernel(a_ref, wt_ref, wz_ref, o_ref, z_ref):
    o_ref[...] = _dot_nt(a_ref[...], wt_ref[...].astype(BF16)).astype(o_ref.dtype)

    @pl.when(pl.program_id(1) == 0)
    def _():
        z_ref[...] = _dot_nt(a_ref[...], wz_ref[...].astype(BF16))


def _in_proj(a, w_in_t, *, tm, tn):
    m, k = a.shape

    def row_start(j):
        return pl.multiple_of(j * tn + jnp.where(j * tn >= N_MAIN, GLA_RANK, 0), GLA_RANK)

    return pl.pallas_call(
        _in_proj_kernel,
        out_shape=(jax.ShapeDtypeStruct((m, N_PROJ), BF16), jax.ShapeDtypeStruct((m, GLA_RANK), F32)),
        grid=(m // tm, N_PROJ // tn),
        in_specs=[pl.BlockSpec((tm, k), lambda i, j: (i, 0), pipeline_mode=pl.Buffered(1)),
                  pl.BlockSpec((pl.Element(tn), pl.Element(k)), lambda i, j: (row_start(j), 0)),
                  pl.BlockSpec((GLA_RANK, k), lambda i, j: (N_MAIN // GLA_RANK, 0))],
        out_specs=(pl.BlockSpec((tm, tn), lambda i, j: (i, j)),
                   pl.BlockSpec((tm, GLA_RANK), lambda i, j: (i, 0))),
        compiler_params=_params(("parallel", "arbitrary")),
        name="in_proj",
    )(a, w_in_t, w_in_t)


def _rope_table_kernel(pos_ref, invf_ref, cos_ref, sin_ref):
    ang = pos_ref[...].astype(F32) * invf_ref[...]
    cos_ref[...] = jnp.cos(ang)
    sin_ref[...] = jnp.sin(ang)


def _rope_tables(pos_col, inv_freq, *, rows):
    m = pos_col.shape[0]
    half = inv_freq.shape[1]
    out = jax.ShapeDtypeStruct((m, half), F32)
    return pl.pallas_call(
        _rope_table_kernel,
        out_shape=(out, out),
        grid=(m // rows,),
        in_specs=[pl.BlockSpec((rows, 1), lambda i: (i, 0)),
                  pl.BlockSpec((1, half), lambda i: (0, 0))],
        out_specs=(pl.BlockSpec((rows, half), lambda i: (i, 0)),
                   pl.BlockSpec((rows, half), lambda i: (i, 0))),
        compiler_params=_params(("parallel",)),
        name="rope_tables",
    )(pos_col, inv_freq)


def _lower_tri(n):
    return (_iota((n, n), 0) >= _iota((n, n), 1)).astype(BF16)


def _gla_log_decay(rows, tri, z_ref, wgu_ref, bg_ref):
    u = _dot(z_ref[rows, :].astype(BF16), wgu_ref[...]) + bg_ref[...]
    log_a = (jnp.minimum(u, 0.0) - jnp.log(1.0 + jnp.exp(-jnp.abs(u)))) * (1.0 / GLA_GATE_TAU)
    hi = log_a.astype(BF16)
    r1 = log_a - hi.astype(F32)
    mid = r1.astype(BF16)
    lo = (r1 - mid.astype(F32)).astype(BF16)
    return _dot(tri, hi) + _dot(tri, mid) + _dot(tri, lo)


def _gla_finish(rows, y, q, k, v, b, g_ref, out_ref, s_ref):
    n = b.shape[0]
    b_last = b[n - 1:n, :]
    kd = k * jnp.exp(b_last - b)
    upd = _dot_tn(kd.astype(BF16), v)
    decay_col = jnp.broadcast_to(jnp.exp(b_last), (LANES, GLA_DK)).T
    s_ref[...] = s_ref[...] * jnp.tile(decay_col, (1, GLA_DV // LANES)) + upd
    yn = y * lax.rsqrt(jnp.mean(y * y, axis=-1, keepdims=True) + EPS) * g_ref[...]
    out_ref[rows, :] = yn.astype(out_ref.dtype)


def _gla_chunk_direct(r0, n, b, q_ref, k_ref, v_ref, g_ref, out_ref, s_ref):
    rows = pl.ds(r0, n)
    q = q_ref[rows, :].astype(F32) * (GLA_DK ** -0.5)
    k = k_ref[rows, :].astype(F32)
    v = v_ref[rows, :]
    qe = (q * jnp.exp(b)).astype(BF16)
    cross = _dot(qe, s_ref[...].astype(BF16))
    scores = _dot_nt(qe, (k * jnp.exp(-b)).astype(BF16))
    causal = _iota((n, n), 0) >= _iota((n, n), 1)
    y = _dot(jnp.where(causal, scores, 0.0).astype(BF16), v) + cross
    _gla_finish(rows, y, q, k, v, b, g_ref, out_ref, s_ref)


def _gla_chunk_exact(r0, c_len, q_ref, k_ref, v_ref, z_ref, wgu_ref, bg_ref, g_ref, out_ref, s_ref):
    sub = GLA_SUB
    rows = pl.ds(r0, c_len)
    b = _gla_log_decay(rows, _lower_tri(c_len), z_ref, wgu_ref, bg_ref)
    q = q_ref[rows, :].astype(F32) * (GLA_DK ** -0.5)
    k = k_ref[rows, :].astype(F32)
    v = v_ref[rows, :]
    cross = _dot((q * jnp.exp(b)).astype(BF16), s_ref[...].astype(BF16))

    key_row = _iota((c_len, 1), 0)
    lane = _iota((sub, c_len), 1)
    qrow = _iota((sub, c_len), 0)
    blocks = []
    for s in range(c_len // sub):
        sl = slice(sub * s, sub * (s + 1))
        bs, qs, ks = b[sl], q[sl], k[sl]
        acc = jnp.zeros((sub, c_len), F32)
        for j in range(sub):
            w = jnp.exp(jnp.minimum(bs - bs[j:j + 1, :], 0.0))
            col = jnp.sum(qs * w * ks[j:j + 1, :], axis=-1, keepdims=True)
            acc = jnp.where((lane == sub * s + j) & (qrow >= j), col, acc)
        if s > 0:
            b_ref_row = b[sub * s - 1:sub * s, :]
            qt = qs * jnp.exp(bs - b_ref_row)
            earlier = key_row < sub * s
            kt = jnp.where(earlier, k * jnp.exp(jnp.where(earlier, b_ref_row - b, 0.0)), 0.0)
            acc = acc + _dot_nt(qt.astype(BF16), kt.astype(BF16))
        blocks.append(acc)
    scores = jnp.concatenate(blocks, axis=0)
    y = _dot(scores.astype(BF16), v) + cross
    _gla_finish(rows, y, q, k, v, b, g_ref, out_ref, s_ref)


def _chunk_span(b):
    n = b.shape[0]
    return jnp.max(-b[n - 1:n, :], axis=1, keepdims=True)


def _gla_chunk_any(r0, n, q_ref, k_ref, v_ref, z_ref, wgu_ref, bg_ref, g_ref, out_ref, s_ref, b_ref):
    b = b_ref[pl.ds(r0, n), :]
    small = _chunk_span(b)[0, 0] <= GLA_SAFE_SPAN

    @pl.when(small)
    def _():
        _gla_chunk_direct(r0, n, b, q_ref, k_ref, v_ref, g_ref, out_ref, s_ref)

    @pl.when(jnp.logical_not(small))
    def _():
        c_len = min(n, GLA_EXACT_CHUNK)

        def body(i, carry):
            _gla_chunk_exact(pl.multiple_of(r0 + i * c_len, GLA_SUB), c_len,
                             q_ref, k_ref, v_ref, z_ref, wgu_ref, bg_ref, g_ref, out_ref, s_ref)
            return carry

        lax.fori_loop(0, n // c_len, body, 0)


def _gla_kernel(q_ref, k_ref, v_ref, z_ref, wgu_ref, bg_ref, g_ref, out_ref, s_ref, b_ref):
    refs = (q_ref, k_ref, v_ref, z_ref, wgu_ref, bg_ref, g_ref, out_ref, s_ref)
    n_big = (q_ref.shape[0] - N_META) // BIG_CHUNK

    def big_start(c):
        return pl.multiple_of(N_META + c * BIG_CHUNK, N_META)

    b_meta = _gla_log_decay(pl.ds(0, N_META), _lower_tri(N_META), z_ref, wgu_ref, bg_ref)
    b_ref[0:N_META, :] = b_meta
    tri = _lower_tri(BIG_CHUNK)

    def decay_body(c, span):
        rows = pl.ds(big_start(c), BIG_CHUNK)
        b = _gla_log_decay(rows, tri, z_ref, wgu_ref, bg_ref)
        b_ref[rows, :] = b
        return jnp.maximum(span, _chunk_span(b))

    span = lax.fori_loop(0, n_big, decay_body, _chunk_span(b_meta), unroll=True)
    all_small = span[0, 0] <= GLA_SAFE_SPAN

    s_ref[...] = jnp.zeros_like(s_ref)

    @pl.when(all_small)
    def _():
        direct_refs = (q_ref, k_ref, v_ref, g_ref, out_ref, s_ref)
        _gla_chunk_direct(0, N_META, b_ref[0:N_META, :], *direct_refs)

        def body(c, carry):
            r0 = big_start(c)
            _gla_chunk_direct(r0, BIG_CHUNK, b_ref[pl.ds(r0, BIG_CHUNK), :], *direct_refs)
            return carry

        lax.fori_loop(0, n_big, body, 0)

    @pl.when(jnp.logical_not(all_small))
    def _():
        _gla_chunk_any(0, N_META, *refs, b_ref)

        def body(c, carry):
            _gla_chunk_any(big_start(c), BIG_CHUNK, *refs, b_ref)
            return carry

        lax.fori_loop(0, n_big, body, 0)


def _gla(proj, z, wgu, bg, g_gla, *, batch, rows):
    m = proj.shape[0]
    return pl.pallas_call(
        _gla_kernel,
        out_shape=jax.ShapeDtypeStruct((m, D_MODEL), BF16),
        grid=(batch, GLA_HEADS),
        in_specs=[
            pl.BlockSpec((rows, GLA_DK), lambda b, h: (b, COL_Q_G // GLA_DK + h)),
            pl.BlockSpec((rows, GLA_DK), lambda b, h: (b, COL_K_G // GLA_DK + h)),
            pl.BlockSpec((rows, GLA_DV), lambda b, h: (b, COL_V_G // GLA_DV + h)),
            pl.BlockSpec((rows, GLA_RANK), lambda b, h: (b, 0)),
            pl.BlockSpec((GLA_RANK, GLA_DK), lambda b, h: (0, h)),
            pl.BlockSpec((1, GLA_DK), lambda b, h: (0, h)),
            pl.BlockSpec((1, GLA_DV), lambda b, h: (0, h)),
        ],
        out_specs=pl.BlockSpec((rows, GLA_DV), lambda b, h: (b, h)),
        scratch_shapes=[pltpu.VMEM((GLA_DK, GLA_DV), F32), pltpu.VMEM((rows, GLA_DK), F32)],
        compiler_params=_params(("parallel", "parallel")),
        name="gla",
    )(proj, proj, proj, z, wgu, bg, g_gla.reshape(1, D_MODEL))


def _ret_decays(lg, n):
    rel = (_iota((n, n), 0) - _iota((n, n), 1)).astype(F32)
    d_intra = jnp.where(rel >= 0, jnp.exp(lg * jnp.maximum(rel, 0.0)), 0.0)
    ridx = _iota((n, 1), 0).astype(F32)
    d_q = jnp.exp(lg * (ridx + 1.0))
    d_k = jnp.exp(lg * (n - 1.0 - ridx))
    d_chunk = jnp.exp(lg * float(n))
    return d_intra, d_q, d_k, d_chunk


def _ret_chunk(r0, decays, q_ref, k_ref, v_ref, o_ref, m_ref, cos_ref, sin_ref, g_ref,
               yg_ref, og_ref, mg_ref, out_ref, s_ref):
    d_intra, d_q, d_k, d_chunk = decays
    n = d_intra.shape[0]
    rows = pl.ds(r0, n)
    half = RET_DK // 2
    cos = cos_ref[rows, :]
    sin = sin_ref[rows, :]

    def rope(ref):
        x = ref[rows, :].astype(F32)
        x1, x2 = x[:, :half], x[:, half:]
        return jnp.concatenate([x1 * cos - x2 * sin, x2 * cos + x1 * sin], axis=-1)

    q = rope(q_ref)
    k = rope(k_ref) * (RET_DK ** -0.5)
    v = v_ref[rows, :]

    qb = q.astype(BF16)
    scores = _dot_nt(qb, k.astype(BF16)) * d_intra
    state = s_ref[...]
    y = _dot(scores.astype(BF16), v) + _dot(qb, state.astype(BF16)) * d_q
    s_ref[...] = state * d_chunk + _dot_tn((k * d_k).astype(BF16), v)

    mu = jnp.mean(y, axis=-1, keepdims=True)
    yc = y - mu
    var = jnp.mean(yc * yc, axis=-1, keepdims=True)
    yn = yc * lax.rsqrt(var + EPS) * g_ref[...]

    def gated4(branch, o_gate_ref, m_gate_ref):
        o = o_gate_ref[rows, :]
        m = m_gate_ref[rows, :]
        return (1.0 + jnp.tanh(0.5 * m)) * (1.0 + jnp.tanh(0.5 * o)) * (o * branch)

    merged = 0.25 * (gated4(yn.astype(BF16), o_ref, m_ref) + gated4(yg_ref[rows, :], og_ref, mg_ref))
    out_ref[rows, :] = merged.astype(out_ref.dtype)


def _ret_kernel(lg_ref, q_ref, k_ref, v_ref, o_ref, m_ref, cos_ref, sin_ref, g_ref,
                yg_ref, og_ref, mg_ref, out_ref, s_ref):
    lg = lg_ref[0][:, :1]
    refs = (q_ref, k_ref, v_ref, o_ref, m_ref, cos_ref, sin_ref, g_ref, yg_ref, og_ref, mg_ref,
            out_ref, s_ref)
    s_ref[...] = jnp.zeros_like(s_ref)
    _ret_chunk(0, _ret_decays(lg, N_META), *refs)
    decays = _ret_decays(lg, BIG_CHUNK)

    def body(c, carry):
        r0 = pl.multiple_of(N_META + c * BIG_CHUNK, N_META)
        _ret_chunk(r0, decays, *refs)
        return carry

    lax.fori_loop(0, (q_ref.shape[0] - N_META) // BIG_CHUNK, body, 0)


def _retention_merge(proj, y_gla, cos, sin, g_ret, *, batch, rows):
    m = proj.shape[0]
    log_gamma = jnp.log1p(-jnp.exp2(-5.0 - jnp.arange(RET_HEADS, dtype=F32)))
    lg = jnp.broadcast_to(log_gamma[:, None, None], (RET_HEADS, 1, LANES))

    def cols(start, width):
        return lambda b, h: (b, start // width + h)

    dk, dv = RET_DK, RET_DV
    return pl.pallas_call(
        _ret_kernel,
        out_shape=jax.ShapeDtypeStruct((m, D_MODEL), BF16),
        grid=(batch, RET_HEADS),
        in_specs=[
            pl.BlockSpec((1, 1, LANES), lambda b, h: (h, 0, 0)),
            pl.BlockSpec((rows, dk), cols(COL_Q_R, dk)),
            pl.BlockSpec((rows, dk), cols(COL_K_R, dk)),
            pl.BlockSpec((rows, dv), cols(COL_V_R, dv)),
            pl.BlockSpec((rows, dv), cols(COL_O_R, dv)),
            pl.BlockSpec((rows, dv), cols(COL_M_R, dv)),
            pl.BlockSpec((rows, dk // 2), lambda b, h: (b, 0)),
            pl.BlockSpec((rows, dk // 2), lambda b, h: (b, 0)),
            pl.BlockSpec((1, dv), lambda b, h: (0, h)),
            pl.BlockSpec((rows, dv), lambda b, h: (b, h)),
            pl.BlockSpec((rows, dv), cols(COL_O_G, dv)),
            pl.BlockSpec((rows, dv), cols(COL_M_G, dv)),
        ],
        out_specs=pl.BlockSpec((rows, dv), lambda b, h: (b, h)),
        scratch_shapes=[pltpu.VMEM((dk, dv), F32)],
        compiler_params=_params(("parallel", "parallel")),
        name="retention_merge",
    )(lg, proj, proj, proj, proj, proj, cos, sin, g_ret.reshape(1, D_MODEL), y_gla, proj, proj)


def _ffn_in_kernel(a_ref, wu_ref, wg_ref, cw_ref, cb_ref, wo_ref, hid_ref, wo_bf16_ref):
    tn_out = wo_bf16_ref.shape[2]
    for t in range(wo_bf16_ref.shape[0]):
        wo_bf16_ref[t] = wo_ref[:, t * tn_out:(t + 1) * tn_out].astype(BF16)
    rows = a_ref.shape[0]
    tf = wu_ref.shape[1]
    w = jnp.concatenate([wu_ref[...].astype(BF16), wg_ref[...].astype(BF16)], axis=1)
    cw = 0.5 * cw_ref[...]
    cb = 0.5 * cb_ref[...]
    tail = jnp.zeros((8, tf), F32)
    starts = list(range(0, rows - FFN_ROW_CHUNK + 1, FFN_ROW_CHUNK))
    for lo, hi in zip(starts, starts[1:] + [rows]):
        up_gate = _dot(a_ref[lo:hi, :], w)
        up, gate = up_gate[:, :tf], up_gate[:, tf:]
        ext = jnp.concatenate([tail, up], axis=0)
        up_m1 = pltpu.roll(ext, 1, axis=0)[8:]
        up_m2 = pltpu.roll(ext, 2, axis=0)[8:]
        ch = cb + cw[0:1, :] * up_m2 + cw[1:2, :] * up_m1 + cw[2:3, :] * up
        hid_ref[lo:hi, :] = (ch * (1.0 + jnp.tanh(ch)) * gate).astype(hid_ref.dtype)
        tail = up[hi - lo - 8:]


def _ffn_in(a, w_ffn_in, conv_w, conv_b, w_ffn_out, *, rows, tf, tn_out):
    m, k = a.shape
    nf = D_FF // tf
    n_steps = (m // rows) * nf
    d_out = w_ffn_out.shape[1]
    slab = D_FF // n_steps
    assert slab * n_steps == D_FF and slab % 16 == 0
    return pl.pallas_call(
        _ffn_in_kernel,
        out_shape=(jax.ShapeDtypeStruct((m, D_FF), BF16),
                   jax.ShapeDtypeStruct((d_out // tn_out, D_FF, tn_out), BF16)),
        grid=(m // rows, nf),
        in_specs=[
            pl.BlockSpec((rows, k), lambda i, j: (i, 0), pipeline_mode=pl.Buffered(1)),
            pl.BlockSpec((k, tf), lambda i, j: (0, j)),
            pl.BlockSpec((k, tf), lambda i, j: (0, nf + j)),
            pl.BlockSpec((CONV_W, tf), lambda i, j: (0, j)),
            pl.BlockSpec((1, tf), lambda i, j: (0, j)),
            pl.BlockSpec((slab, d_out), lambda i, j: (i * nf + j, 0)),
        ],
        out_specs=(pl.BlockSpec((rows, tf), lambda i, j: (i, j)),
                   pl.BlockSpec((d_out // tn_out, slab, tn_out), lambda i, j: (0, i * nf + j, 0))),
        compiler_params=_params(("parallel", "parallel")),
        name="ffn_in_conv",
    )(a, w_ffn_in, w_ffn_in, conv_w, conv_b.reshape(1, D_FF), w_ffn_out)


def kernel(x, positions, meta_tokens, attn_norm, w_in, w_gate_up, b_gate, ret_norm, gla_norm,
           w_out, ffn_norm, w_ffn_in, conv_w, conv_b, w_ffn_out, final_norm):
    batch, seq, d = x.shape
    rows = N_META + seq
    m = batch * rows

    meta_tokens = meta_tokens.astype(x.dtype)
    pos = jnp.concatenate([
        jnp.broadcast_to(jnp.arange(N_META, dtype=jnp.int32), (batch, N_META)),
        positions.astype(jnp.int32) + N_META], axis=1).reshape(m, 1)
    half = RET_DK // 2
    inv_freq = (ROPE_BASE ** (-jnp.arange(half, dtype=F32) / half)).reshape(1, half)

    w_in_t = jnp.swapaxes(w_in, 1, 2)[0]

    cos, sin = _rope_tables(pos, inv_freq, rows=rows)

    norm_rows = rows // 3
    hn = _embed_norm(x, meta_tokens, attn_norm[0], rows=norm_rows, out_dtype=BF16)
    proj, z = _in_proj(hn, w_in_t, tm=rows, tn=512)

    y_gla = _gla(proj, z, w_gate_up[0].astype(BF16), b_gate[0].reshape(1, -1), gla_norm[0],
                 batch=batch, rows=rows)
    merged = _retention_merge(proj, y_gla, cos, sin, ret_norm[0], batch=batch, rows=rows)

    h1 = _out_proj(merged, w_out[0], x, meta_tokens, tn=256)
    h1n = _rmsnorm(h1, ffn_norm[0], rows=norm_rows, out_dtype=BF16)
    hidden, w_ffn_out_b = _ffn_in(h1n, w_ffn_in[0], conv_w[0], conv_b[0], w_ffn_out[0],
                                  rows=rows, tf=256, tn_out=256)
    h2 = _matmul_res(hidden, w_ffn_out_b, h1, tm=norm_rows, name="ffn_out")
    return _final_norm(h2, final_norm, batch=batch, rows_per_batch=rows, seq=seq, rows=256)
```

```python
import jax
import jax.numpy as jnp
from jax import lax
from jax.experimental import pallas as pl
from jax.experimental.pallas import tpu as pltpu

F32 = jnp.float32
BF16 = jnp.bfloat16

D_MODEL = 4096
N_META = 16
RET_HEADS, RET_DK, RET_DV = 8, 256, 512
GLA_HEADS, GLA_DK, GLA_DV = 4, 512, 1024
GLA_RANK = 16
GLA_GATE_TAU = 16.0
D_FF = 11008
CONV_W = 3
ROPE_BASE = 10000.0
EPS = 1e-6
RET_QK = RET_HEADS * RET_DK
GLA_QK = GLA_HEADS * GLA_DK
COL_Q_R, COL_K_R = 0, RET_QK
COL_V_R, COL_O_R = 2 * RET_QK, 2 * RET_QK + D_MODEL
COL_Q_G = 2 * RET_QK + 2 * D_MODEL
COL_K_G, COL_V_G = COL_Q_G + GLA_QK, COL_Q_G + 2 * GLA_QK
COL_O_G = COL_V_G + D_MODEL
N_MAIN = COL_O_G + D_MODEL
COL_M_R, COL_M_G = N_MAIN, N_MAIN + D_MODEL
N_PROJ = N_MAIN + 2 * D_MODEL

VMEM_LIMIT_V7X = 56 * 1024 * 1024
LANES = 128
BIG_CHUNK = 256
FFN_ROW_CHUNK = 256
GLA_EXACT_CHUNK = 64
GLA_SUB = 16
GLA_SAFE_SPAN = 60.0


def _params(sem):
    return pltpu.CompilerParams(dimension_semantics=sem, vmem_limit_bytes=VMEM_LIMIT_V7X)


def _iota(shape, dim):
    return lax.broadcasted_iota(jnp.int32, shape, dim)


def _dot(a, b):
    return jnp.dot(a, b, preferred_element_type=F32)


def _dot_nt(a, b):
    return lax.dot_general(a, b, (((1,), (1,)), ((), ())), preferred_element_type=F32)


def _dot_tn(a, b):
    return lax.dot_general(a, b, (((0,), (0,)), ((), ())), preferred_element_type=F32)


def _rmsnorm_kernel(x_ref, g_ref, o_ref):
    x = x_ref[...]
    y = x * lax.rsqrt(jnp.mean(x * x, axis=-1, keepdims=True) + EPS)
    o_ref[...] = (y * g_ref[...]).astype(o_ref.dtype)


def _rmsnorm(x, g, *, rows, out_dtype):
    m, d = x.shape
    return pl.pallas_call(
        _rmsnorm_kernel,
        out_shape=jax.ShapeDtypeStruct((m, d), out_dtype),
        grid=(m // rows,),
        in_specs=[pl.BlockSpec((rows, d), lambda i: (i, 0)),
                  pl.BlockSpec((1, d), lambda i: (0, 0))],
        out_specs=pl.BlockSpec((rows, d), lambda i: (i, 0)),
        compiler_params=_params(("parallel",)),
        name="rmsnorm",
    )(x, g.reshape(1, d))


def _embed_norm_kernel(x_ref, meta_ref, g_ref, o_ref):
    def norm(v):
        y = v * lax.rsqrt(jnp.mean(v * v, axis=-1, keepdims=True) + EPS)
        return (y * g_ref[...]).astype(o_ref.dtype)

    r = pl.program_id(1)
    rows = o_ref.shape[0]

    @pl.when(r == 0)
    def _():
        o_ref[0:N_META, :] = norm(meta_ref[...])
        o_ref[N_META:, :] = norm(x_ref[0, 0:rows - N_META, :])

    @pl.when(r > 0)
    def _():
        o_ref[...] = norm(x_ref[0])


def _embed_norm(x, meta_tokens, g, *, rows, out_dtype):
    batch, seq, d = x.shape
    tiles = (N_META + seq) // rows

    def x_start(r):
        return pl.multiple_of(jnp.maximum(rows * r - N_META, 0), N_META)

    return pl.pallas_call(
        _embed_norm_kernel,
        out_shape=jax.ShapeDtypeStruct((batch * (N_META + seq), d), out_dtype),
        grid=(batch, tiles),
        in_specs=[pl.BlockSpec((pl.Element(1), pl.Element(rows), pl.Element(d)),
                               lambda b, r: (b, x_start(r), 0)),
                  pl.BlockSpec((N_META, d), lambda b, r: (0, 0)),
                  pl.BlockSpec((1, d), lambda b, r: (0, 0))],
        out_specs=pl.BlockSpec((rows, d), lambda b, r: (b * tiles + r, 0)),
        compiler_params=_params(("parallel", "parallel")),
        name="embed_norm",
    )(x, meta_tokens, g.reshape(1, d))


def _final_norm_kernel(x_ref, g_ref, o_ref):
    x = x_ref[...]
    y = x * lax.rsqrt(jnp.mean(x * x, axis=-1, keepdims=True) + EPS)
    o_ref[...] = y * g_ref[...]


def _final_norm(h, g, *, batch, rows_per_batch, seq, rows):
    d = h.shape[1]
    lead = rows_per_batch - seq
    return pl.pallas_call(
        _final_norm_kernel,
        out_shape=jax.ShapeDtypeStruct((batch, seq, d), F32),
        grid=(batch, seq // rows),
        in_specs=[pl.BlockSpec((pl.Element(1), pl.Element(rows), pl.Element(d)),
                               lambda b, j: (b, pl.multiple_of(lead + rows * j, N_META), 0)),
                  pl.BlockSpec((1, d), lambda b, j: (0, 0))],
        out_specs=pl.BlockSpec((1, rows, d), lambda b, j: (b, j, 0)),
        compiler_params=_params(("parallel", "parallel")),
        name="final_norm",
    )(h.reshape(batch, rows_per_batch, d), g.reshape(1, d))


def _mm_res_kernel(a_ref, w_ref, r_ref, o_ref):
    o_ref[...] = r_ref[...] + _dot(a_ref[...], w_ref[...])


def _matmul_res(a, w, residual, *, tm, tn, name):
    m, k = a.shape
    n = w.shape[1]
    return pl.pallas_call(
        _mm_res_kernel,
        out_shape=jax.ShapeDtypeStruct((m, n), F32),
        grid=(m // tm, n // tn),
        in_specs=[pl.BlockSpec((tm, k), lambda i, j: (i, 0)),
                  pl.BlockSpec((k, tn), lambda i, j: (0, j)),
                  pl.BlockSpec((tm, tn), lambda i, j: (i, j))],
        out_specs=pl.BlockSpec((tm, tn), lambda i, j: (i, j)),
        compiler_params=_params(("parallel", "parallel")),
        name=name,
    )(a, w, residual)


def _out_proj_kernel(a_ref, w_ref, x_ref, meta_ref, o_ref):
    w = w_ref[...].astype(BF16)
    seq = x_ref.shape[1]
    acc = _dot(a_ref[0:N_META + BIG_CHUNK, :], w)
    o_ref[0:N_META, :] = meta_ref[...] + acc[0:N_META]
    o_ref[N_META:N_META + BIG_CHUNK, :] = x_ref[0, 0:BIG_CHUNK, :] + acc[N_META:]
    for lo in range(BIG_CHUNK, seq, BIG_CHUNK):
        rows = slice(N_META + lo, N_META + lo + BIG_CHUNK)
        o_ref[rows, :] = x_ref[0, lo:lo + BIG_CHUNK, :] + _dot(a_ref[rows, :], w)


def _out_proj(a, w, x, meta_tokens, *, tn):
    batch, seq, d = x.shape
    m, k = a.shape
    rows = N_META + seq
    return pl.pallas_call(
        _out_proj_kernel,
        out_shape=jax.ShapeDtypeStruct((m, d), F32),
        grid=(batch, d // tn),
        in_specs=[pl.BlockSpec((rows, k), lambda b, j: (b, 0)),
                  pl.BlockSpec((k, tn), lambda b, j: (0, j)),
                  pl.BlockSpec((1, seq, tn), lambda b, j: (b, 0, j)),
                  pl.BlockSpec((N_META, tn), lambda b, j: (0, j))],
        out_specs=pl.BlockSpec((rows, tn), lambda b, j: (b, j)),
        compiler_params=_params(("parallel", "parallel")),
        name="out_proj",
    )(a, w, x, meta_tokens)


def _in_proj_kernel(a_ref, wt_ref, wz_ref, o_ref, z_ref):
    o_ref[...] = _dot_nt(a_ref[...], wt_ref[...].astype(BF16)).astype(o_ref.dtype)

    @pl.when(pl.program_id(1) == 0)
    def _():
        z_ref[...] = _dot_nt(a_ref[...], wz_ref[...].astype(BF16))


def _in_proj(a, w_in_t, *, tm, tn):
    m, k = a.shape

    def row_start(j):
        return pl.multiple_of(j * tn + jnp.where(j * tn >= N_MAIN, GLA_RANK, 0), GLA_RANK)

    return pl.pallas_call(
        _in_proj_kernel,
        out_shape=(jax.ShapeDtypeStruct((m, N_PROJ), BF16), jax.ShapeDtypeStruct((m, GLA_RANK), F32)),
        grid=(m // tm, N_PROJ // tn),
        in_specs=[pl.BlockSpec((tm, k), lambda i, j: (i, 0), pipeline_mode=pl.Buffered(1)),
                  pl.BlockSpec((pl.Element(tn), pl.Element(k)), lambda i, j: (row_start(j), 0)),
                  pl.BlockSpec((GLA_RANK, k), lambda i, j: (N_MAIN // GLA_RANK, 0))],
        out_specs=(pl.BlockSpec((tm, tn), lambda i, j: (i, j)),
                   pl.BlockSpec((tm, GLA_RANK), lambda i, j: (i, 0))),
        compiler_params=_params(("parallel", "arbitrary")),
        name="in_proj",
    )(a, w_in_t, w_in_t)


def _rope_table_kernel(pos_ref, invf_ref, cos_ref, sin_ref):
    ang = pos_ref[...].astype(F32) * invf_ref[...]
    cos_ref[...] = jnp.cos(ang)
    sin_ref[...] = jnp.sin(ang)


def _rope_tables(pos_col, inv_freq, *, rows):
    m = pos_col.shape[0]
    half = inv_freq.shape[1]
    out = jax.ShapeDtypeStruct((m, half), F32)
    return pl.pallas_call(
        _rope_table_kernel,
        out_shape=(out, out),
        grid=(m // rows,),
        in_specs=[pl.BlockSpec((rows, 1), lambda i: (i, 0)),
                  pl.BlockSpec((1, half), lambda i: (0, 0))],
        out_specs=(pl.BlockSpec((rows, half), lambda i: (i, 0)),
                   pl.BlockSpec((rows, half), lambda i: (i, 0))),
        compiler_params=_params(("parallel",)),
        name="rope_tables",
    )(pos_col, inv_freq)


def _lower_tri(n):
    return (_iota((n, n), 0) >= _iota((n, n), 1)).astype(BF16)


def _gla_log_decay(rows, tri, z_ref, wgu_ref, bg_ref):
    u = _dot(z_ref[rows, :].astype(BF16), wgu_ref[...]) + bg_ref[...]
    log_a = (jnp.minimum(u, 0.0) - jnp.log(1.0 + jnp.exp(-jnp.abs(u)))) * (1.0 / GLA_GATE_TAU)
    hi = log_a.astype(BF16)
    r1 = log_a - hi.astype(F32)
    mid = r1.astype(BF16)
    lo = (r1 - mid.astype(F32)).astype(BF16)
    return _dot(tri, hi) + _dot(tri, mid) + _dot(tri, lo)


def _gla_finish(rows, y, q, k, v, b, g_ref, out_ref, s_ref):
    n = b.shape[0]
    b_last = b[n - 1:n, :]
    kd = k * jnp.exp(b_last - b)
    upd = _dot_tn(kd.astype(BF16), v)
    decay_col = jnp.broadcast_to(jnp.exp(b_last), (LANES, GLA_DK)).T
    s_ref[...] = s_ref[...] * jnp.tile(decay_col, (1, GLA_DV // LANES)) + upd
    yn = y * lax.rsqrt(jnp.mean(y * y, axis=-1, keepdims=True) + EPS) * g_ref[...]
    out_ref[rows, :] = yn.astype(out_ref.dtype)


def _gla_chunk_direct(r0, n, b, q_ref, k_ref, v_ref, g_ref, out_ref, s_ref):
    rows = pl.ds(r0, n)
    q = q_ref[rows, :].astype(F32) * (GLA_DK ** -0.5)
    k = k_ref[rows, :].astype(F32)
    v = v_ref[rows, :]
    qe = (q * jnp.exp(b)).astype(BF16)
    cross = _dot(qe, s_ref[...].astype(BF16))
    scores = _dot_nt(qe, (k * jnp.exp(-b)).astype(BF16))
    causal = _iota((n, n), 0) >= _iota((n, n), 1)
    y = _dot(jnp.where(causal, scores, 0.0).astype(BF16), v) + cross
    _gla_finish(rows, y, q, k, v, b, g_ref, out_ref, s_ref)


def _gla_chunk_exact(r0, c_len, q_ref, k_ref, v_ref, z_ref, wgu_ref, bg_ref, g_ref, out_ref, s_ref):
    sub = GLA_SUB
    rows = pl.ds(r0, c_len)
    b = _gla_log_decay(rows, _lower_tri(c_len), z_ref, wgu_ref, bg_ref)
    q = q_ref[rows, :].astype(F32) * (GLA_DK ** -0.5)
    k = k_ref[rows, :].astype(F32)
    v = v_ref[rows, :]
    cross = _dot((q * jnp.exp(b)).astype(BF16), s_ref[...].astype(BF16))

    key_row = _iota((c_len, 1), 0)
    lane = _iota((sub, c_len), 1)
    qrow = _iota((sub, c_len), 0)
    blocks = []
    for s in range(c_len // sub):
        sl = slice(sub * s, sub * (s + 1))
        bs, qs, ks = b[sl], q[sl], k[sl]
        acc = jnp.zeros((sub, c_len), F32)
        for j in range(sub):
            w = jnp.exp(jnp.minimum(bs - bs[j:j + 1, :], 0.0))
            col = jnp.sum(qs * w * ks[j:j + 1, :], axis=-1, keepdims=True)
            acc = jnp.where((lane == sub * s + j) & (qrow >= j), col, acc)
        if s > 0:
            b_ref_row = b[sub * s - 1:sub * s, :]
            qt = qs * jnp.exp(bs - b_ref_row)
            earlier = key_row < sub * s
            kt = jnp.where(earlier, k * jnp.exp(jnp.where(earlier, b_ref_row - b, 0.0)), 0.0)
            acc = acc + _dot_nt(qt.astype(BF16), kt.astype(BF16))
        blocks.append(acc)
    scores = jnp.concatenate(blocks, axis=0)
    y = _dot(scores.astype(BF16), v) + cross
    _gla_finish(rows, y, q, k, v, b, g_ref, out_ref, s_ref)


def _chunk_span(b):
    n = b.shape[0]
    return jnp.max(-b[n - 1:n, :], axis=1, keepdims=True)


def _gla_chunk_any(r0, n, q_ref, k_ref, v_ref, z_ref, wgu_ref, bg_ref, g_ref, out_ref, s_ref, b_ref):
    b = b_ref[pl.ds(r0, n), :]
    small = _chunk_span(b)[0, 0] <= GLA_SAFE_SPAN

    @pl.when(small)
    def _():
        _gla_chunk_direct(r0, n, b, q_ref, k_ref, v_ref, g_ref, out_ref, s_ref)

    @pl.when(jnp.logical_not(small))
    def _():
        c_len = min(n, GLA_EXACT_CHUNK)

        def body(i, carry):
            _gla_chunk_exact(pl.multiple_of(r0 + i * c_len, GLA_SUB), c_len,
                             q_ref, k_ref, v_ref, z_ref, wgu_ref, bg_ref, g_ref, out_ref, s_ref)
            return carry

        lax.fori_loop(0, n // c_len, body, 0)


def _gla_kernel(q_ref, k_ref, v_ref, z_ref, wgu_ref, bg_ref, g_ref, out_ref, s_ref, b_ref):
    refs = (q_ref, k_ref, v_ref, z_ref, wgu_ref, bg_ref, g_ref, out_ref, s_ref)
    n_big = (q_ref.shape[0] - N_META) // BIG_CHUNK

    def big_start(c):
        return pl.multiple_of(N_META + c * BIG_CHUNK, N_META)

    b_meta = _gla_log_decay(pl.ds(0, N_META), _lower_tri(N_META), z_ref, wgu_ref, bg_ref)
    b_ref[0:N_META, :] = b_meta
    tri = _lower_tri(BIG_CHUNK)

    def decay_body(c, span):
        rows = pl.ds(big_start(c), BIG_CHUNK)
        b = _gla_log_decay(rows, tri, z_ref, wgu_ref, bg_ref)
        b_ref[rows, :] = b
        return jnp.maximum(span, _chunk_span(b))

    span = lax.fori_loop(0, n_big, decay_body, _chunk_span(b_meta), unroll=True)
    all_small = span[0, 0] <= GLA_SAFE_SPAN

    s_ref[...] = jnp.zeros_like(s_ref)

    @pl.when(all_small)
    def _():
        direct_refs = (q_ref, k_ref, v_ref, g_ref, out_ref, s_ref)
        _gla_chunk_direct(0, N_META, b_ref[0:N_META, :], *direct_refs)

        def body(c, carry):
            r0 = big_start(c)
            _gla_chunk_direct(r0, BIG_CHUNK, b_ref[pl.ds(r0, BIG_CHUNK), :], *direct_refs)
            return carry

        lax.fori_loop(0, n_big, body, 0)

    @pl.when(jnp.logical_not(all_small))
    def _():
        _gla_chunk_any(0, N_META, *refs, b_ref)

        def body(c, carry):
            _gla_chunk_any(big_start(c), BIG_CHUNK, *refs, b_ref)
            return carry

        lax.fori_loop(0, n_big, body, 0)


def _gla(proj, z, wgu, bg, g_gla, *, batch, rows):
    m = proj.shape[0]
    return pl.pallas_call(
        _gla_kernel,
        out_shape=jax.ShapeDtypeStruct((m, D_MODEL), BF16),
        grid=(batch, GLA_HEADS),
        in_specs=[
            pl.BlockSpec((rows, GLA_DK), lambda b, h: (b, COL_Q_G // GLA_DK + h)),
            pl.BlockSpec((rows, GLA_DK), lambda b, h: (b, COL_K_G // GLA_DK + h)),
            pl.BlockSpec((rows, GLA_DV), lambda b, h: (b, COL_V_G // GLA_DV + h)),
            pl.BlockSpec((rows, GLA_RANK), lambda b, h: (b, 0)),
            pl.BlockSpec((GLA_RANK, GLA_DK), lambda b, h: (0, h)),
            pl.BlockSpec((1, GLA_DK), lambda b, h: (0, h)),
            pl.BlockSpec((1, GLA_DV), lambda b, h: (0, h)),
        ],
        out_specs=pl.BlockSpec((rows, GLA_DV), lambda b, h: (b, h)),
        scratch_shapes=[pltpu.VMEM((GLA_DK, GLA_DV), F32), pltpu.VMEM((rows, GLA_DK), F32)],
        compiler_params=_params(("parallel", "parallel")),
        name="gla",
    )(proj, proj, proj, z, wgu, bg, g_gla.reshape(1, D_MODEL))


def _ret_decays(lg, n):
    rel = (_iota((n, n), 0) - _iota((n, n), 1)).astype(F32)
    d_intra = jnp.where(rel >= 0, jnp.exp(lg * jnp.maximum(rel, 0.0)), 0.0)
    ridx = _iota((n, 1), 0).astype(F32)
    d_q = jnp.exp(lg * (ridx + 1.0))
    d_k = jnp.exp(lg * (n - 1.0 - ridx))
    d_chunk = jnp.exp(lg * float(n))
    return d_intra, d_q, d_k, d_chunk


def _ret_chunk(r0, decays, q_ref, k_ref, v_ref, o_ref, m_ref, cos_ref, sin_ref, g_ref,
               yg_ref, og_ref, mg_ref, out_ref, s_ref):
    d_intra, d_q, d_k, d_chunk = decays
    n = d_intra.shape[0]
    rows = pl.ds(r0, n)
    half = RET_DK // 2
    cos = cos_ref[rows, :]
    sin = sin_ref[rows, :]

    def rope(ref):
        x = ref[rows, :].astype(F32)
        x1, x2 = x[:, :half], x[:, half:]
        return jnp.concatenate([x1 * cos - x2 * sin, x2 * cos + x1 * sin], axis=-1)

    q = rope(q_ref)
    k = rope(k_ref) * (RET_DK ** -0.5)
    v = v_ref[rows, :]

    qb = q.astype(BF16)
    scores = _dot_nt(qb, k.astype(BF16)) * d_intra
    state = s_ref[...]
    y = _dot(scores.astype(BF16), v) + _dot(qb, state.astype(BF16)) * d_q
    s_ref[...] = state * d_chunk + _dot_tn((k * d_k).astype(BF16), v)

    mu = jnp.mean(y, axis=-1, keepdims=True)
    yc = y - mu
    var = jnp.mean(yc * yc, axis=-1, keepdims=True)
    yn = yc * lax.rsqrt(var + EPS) * g_ref[...]

    def gated4(branch, o_gate_ref, m_gate_ref):
        o = o_gate_ref[rows, :]
        m = m_gate_ref[rows, :]
        return (1.0 + jnp.tanh(0.5 * m)) * (1.0 + jnp.tanh(0.5 * o)) * (o * branch)

    merged = 0.25 * (gated4(yn.astype(BF16), o_ref, m_ref) + gated4(yg_ref[rows, :], og_ref, mg_ref))
    out_ref[rows, :] = merged.astype(out_ref.dtype)


def _ret_kernel(lg_ref, q_ref, k_ref, v_ref, o_ref, m_ref, cos_ref, sin_ref, g_ref,
                yg_ref, og_ref, mg_ref, out_ref, s_ref):
    lg = lg_ref[0][:, :1]
    refs = (q_ref, k_ref, v_ref, o_ref, m_ref, cos_ref, sin_ref, g_ref, yg_ref, og_ref, mg_ref,
            out_ref, s_ref)
    s_ref[...] = jnp.zeros_like(s_ref)
    _ret_chunk(0, _ret_decays(lg, N_META), *refs)
    decays = _ret_decays(lg, BIG_CHUNK)

    def body(c, carry):
        r0 = pl.multiple_of(N_META + c * BIG_CHUNK, N_META)
        _ret_chunk(r0, decays, *refs)
        return carry

    lax.fori_loop(0, (q_ref.shape[0] - N_META) // BIG_CHUNK, body, 0)


def _retention_merge(proj, y_gla, cos, sin, g_ret, *, batch, rows):
    m = proj.shape[0]
    log_gamma = jnp.log1p(-jnp.exp2(-5.0 - jnp.arange(RET_HEADS, dtype=F32)))
    lg = jnp.broadcast_to(log_gamma[:, None, None], (RET_HEADS, 1, LANES))

    def cols(start, width):
        return lambda b, h: (b, start // width + h)

    dk, dv = RET_DK, RET_DV
    return pl.pallas_call(
        _ret_kernel,
        out_shape=jax.ShapeDtypeStruct((m, D_MODEL), BF16),
        grid=(batch, RET_HEADS),
        in_specs=[
            pl.BlockSpec((1, 1, LANES), lambda b, h: (h, 0, 0)),
            pl.BlockSpec((rows, dk), cols(COL_Q_R, dk)),
            pl.BlockSpec((rows, dk), cols(COL_K_R, dk)),
            pl.BlockSpec((rows, dv), cols(COL_V_R, dv)),
            pl.BlockSpec((rows, dv), cols(COL_O_R, dv)),
            pl.BlockSpec((rows, dv), cols(COL_M_R, dv)),
            pl.BlockSpec((rows, dk // 2), lambda b, h: (b, 0)),
            pl.BlockSpec((rows, dk // 2), lambda b, h: (b, 0)),
            pl.BlockSpec((1, dv), lambda b, h: (0, h)),
            pl.BlockSpec((rows, dv), lambda b, h: (b, h)),
            pl.BlockSpec((rows, dv), cols(COL_O_G, dv)),
            pl.BlockSpec((rows, dv), cols(COL_M_G, dv)),
        ],
        out_specs=pl.BlockSpec((rows, dv), lambda b, h: (b, h)),
        scratch_shapes=[pltpu.VMEM((dk, dv), F32)],
        compiler_params=_params(("parallel", "parallel")),
        name="retention_merge",
    )(lg, proj, proj, proj, proj, proj, cos, sin, g_ret.reshape(1, D_MODEL), y_gla, proj, proj)


def _ffn_in_kernel(a_ref, wu_ref, wg_ref, cw_ref, cb_ref, wo_ref, hid_ref, wo_bf16_ref):
    wo_bf16_ref[...] = wo_ref[...].astype(BF16)
    rows = a_ref.shape[0]
    tf = wu_ref.shape[1]
    w = jnp.concatenate([wu_ref[...].astype(BF16), wg_ref[...].astype(BF16)], axis=1)
    cw = 0.5 * cw_ref[...]
    cb = 0.5 * cb_ref[...]
    tail = jnp.zeros((8, tf), F32)
    starts = list(range(0, rows - FFN_ROW_CHUNK + 1, FFN_ROW_CHUNK))
    for lo, hi in zip(starts, starts[1:] + [rows]):
        up_gate = _dot(a_ref[lo:hi, :], w)
        up, gate = up_gate[:, :tf], up_gate[:, tf:]
        ext = jnp.concatenate([tail, up], axis=0)
        up_m1 = pltpu.roll(ext, 1, axis=0)[8:]
        up_m2 = pltpu.roll(ext, 2, axis=0)[8:]
        ch = cb + cw[0:1, :] * up_m2 + cw[1:2, :] * up_m1 + cw[2:3, :] * up
        hid_ref[lo:hi, :] = (ch * (1.0 + jnp.tanh(ch)) * gate).astype(hid_ref.dtype)
        tail = up[hi - lo - 8:]


def _ffn_in(a, w_ffn_in, conv_w, conv_b, w_ffn_out, *, rows, tf):
    m, k = a.shape
    nf = D_FF // tf
    n_steps = (m // rows) * nf
    d_out = w_ffn_out.shape[1]
    slab = D_FF // n_steps
    assert slab * n_steps == D_FF and slab % 16 == 0
    return pl.pallas_call(
        _ffn_in_kernel,
        out_shape=(jax.ShapeDtypeStruct((m, D_FF), BF16),
                   jax.ShapeDtypeStruct((D_FF, d_out), BF16)),
        grid=(m // rows, nf),
        in_specs=[
            pl.BlockSpec((rows, k), lambda i, j: (i, 0), pipeline_mode=pl.Buffered(1)),
            pl.BlockSpec((k, tf), lambda i, j: (0, j)),
            pl.BlockSpec((k, tf), lambda i, j: (0, nf + j)),
            pl.BlockSpec((CONV_W, tf), lambda i, j: (0, j)),
            pl.BlockSpec((1, tf), lambda i, j: (0, j)),
            pl.BlockSpec((slab, d_out), lambda i, j: (i * nf + j, 0)),
        ],
        out_specs=(pl.BlockSpec((rows, tf), lambda i, j: (i, j)),
                   pl.BlockSpec((slab, d_out), lambda i, j: (i * nf + j, 0))),
        compiler_params=_params(("parallel", "parallel")),
        name="ffn_in_conv",
    )(a, w_ffn_in, w_ffn_in, conv_w, conv_b.reshape(1, D_FF), w_ffn_out)


def kernel(x, positions, meta_tokens, attn_norm, w_in, w_gate_up, b_gate, ret_norm, gla_norm,
           w_out, ffn_norm, w_ffn_in, conv_w, conv_b, w_ffn_out, final_norm):
    batch, seq, d = x.shape
    rows = N_META + seq
    m = batch * rows

    meta_tokens = meta_tokens.astype(x.dtype)
    pos = jnp.concatenate([
        jnp.broadcast_to(jnp.arange(N_META, dtype=jnp.int32), (batch, N_META)),
        positions.astype(jnp.int32) + N_META], axis=1).reshape(m, 1)
    half = RET_DK // 2
    inv_freq = (ROPE_BASE ** (-jnp.arange(half, dtype=F32) / half)).reshape(1, half)

    w_in_t = jnp.swapaxes(w_in, 1, 2)[0]

    cos, sin = _rope_tables(pos, inv_freq, rows=rows)

    norm_rows = rows // 3
    hn = _embed_norm(x, meta_tokens, attn_norm[0], rows=norm_rows, out_dtype=BF16)
    proj, z = _in_proj(hn, w_in_t, tm=rows, tn=512)

    y_gla = _gla(proj, z, w_gate_up[0].astype(BF16), b_gate[0].reshape(1, -1), gla_norm[0],
                 batch=batch, rows=rows)
    merged = _retention_merge(proj, y_gla, cos, sin, ret_norm[0], batch=batch, rows=rows)

    h1 = _out_proj(merged, w_out[0], x, meta_tokens, tn=256)
    h1n = _rmsnorm(h1, ffn_norm[0], rows=norm_rows, out_dtype=BF16)
    hidden, w_ffn_out_b = _ffn_in(h1n, w_ffn_in[0], conv_w[0], conv_b[0], w_ffn_out[0],
                                  rows=rows, tf=256)
    h2 = _matmul_res(hidden, w_ffn_out_b, h1, tm=norm_rows, tn=256, name="ffn_out")
    return _final_norm(h2, final_norm, batch=batch, rows_per_batch=rows, seq=seq, rows=512)
```

```python
import jax
import jax.numpy as jnp
from jax import lax
from jax.experimental import pallas as pl
from jax.experimental.pallas import tpu as pltpu

F32 = jnp.float32
BF16 = jnp.bfloat16

D_MODEL = 4096
N_META = 16
RET_HEADS, RET_DK, RET_DV = 8, 256, 512
GLA_HEADS, GLA_DK, GLA_DV = 4, 512, 1024
GLA_RANK = 16
GLA_GATE_TAU = 16.0
D_FF = 11008
CONV_W = 3
ROPE_BASE = 10000.0
EPS = 1e-6
RET_QK = RET_HEADS * RET_DK
GLA_QK = GLA_HEADS * GLA_DK
COL_Q_R, COL_K_R = 0, RET_QK
COL_V_R, COL_O_R = 2 * RET_QK, 2 * RET_QK + D_MODEL
COL_Q_G = 2 * RET_QK + 2 * D_MODEL
COL_K_G, COL_V_G = COL_Q_G + GLA_QK, COL_Q_G + 2 * GLA_QK
COL_O_G = COL_V_G + D_MODEL
N_MAIN = COL_O_G + D_MODEL
COL_M_R, COL_M_G = N_MAIN, N_MAIN + D_MODEL
N_PROJ = N_MAIN + 2 * D_MODEL

VMEM_LIMIT_V7X = 56 * 1024 * 1024
LANES = 128
BIG_CHUNK = 256
FFN_ROW_CHUNK = 256
GLA_EXACT_CHUNK = 64
GLA_SUB = 16
GLA_SAFE_SPAN = 60.0


def _params(sem):
    return pltpu.CompilerParams(dimension_semantics=sem, vmem_limit_bytes=VMEM_LIMIT_V7X)


def _iota(shape, dim):
    return lax.broadcasted_iota(jnp.int32, shape, dim)


def _dot(a, b):
    return jnp.dot(a, b, preferred_element_type=F32)


def _dot_nt(a, b):
    return lax.dot_general(a, b, (((1,), (1,)), ((), ())), preferred_element_type=F32)


def _dot_tn(a, b):
    return lax.dot_general(a, b, (((0,), (0,)), ((), ())), preferred_element_type=F32)


def _rmsnorm_kernel(x_ref, g_ref, o_ref):
    x = x_ref[...]
    y = x * lax.rsqrt(jnp.mean(x * x, axis=-1, keepdims=True) + EPS)
    o_ref[...] = (y * g_ref[...]).astype(o_ref.dtype)


def _rmsnorm(x, g, *, rows, out_dtype):
    m, d = x.shape
    return pl.pallas_call(
        _rmsnorm_kernel,
        out_shape=jax.ShapeDtypeStruct((m, d), out_dtype),
        grid=(m // rows,),
        in_specs=[pl.BlockSpec((rows, d), lambda i: (i, 0)),
                  pl.BlockSpec((1, d), lambda i: (0, 0))],
        out_specs=pl.BlockSpec((rows, d), lambda i: (i, 0)),
        compiler_params=_params(("parallel",)),
        name="rmsnorm",
    )(x, g.reshape(1, d))


def _embed_norm_kernel(x_ref, meta_ref, g_ref, o_ref):
    def norm(v):
        y = v * lax.rsqrt(jnp.mean(v * v, axis=-1, keepdims=True) + EPS)
        return (y * g_ref[...]).astype(o_ref.dtype)

    r = pl.program_id(1)
    rows = o_ref.shape[0]

    @pl.when(r == 0)
    def _():
        o_ref[0:N_META, :] = norm(meta_ref[...])
        o_ref[N_META:, :] = norm(x_ref[0, 0:rows - N_META, :])

    @pl.when(r > 0)
    def _():
        o_ref[...] = norm(x_ref[0])


def _embed_norm(x, meta_tokens, g, *, rows, out_dtype):
    batch, seq, d = x.shape
    tiles = (N_META + seq) // rows

    def x_start(r):
        return pl.multiple_of(jnp.maximum(rows * r - N_META, 0), N_META)

    return pl.pallas_call(
        _embed_norm_kernel,
        out_shape=jax.ShapeDtypeStruct((batch * (N_META + seq), d), out_dtype),
        grid=(batch, tiles),
        in_specs=[pl.BlockSpec((pl.Element(1), pl.Element(rows), pl.Element(d)),
                               lambda b, r: (b, x_start(r), 0)),
                  pl.BlockSpec((N_META, d), lambda b, r: (0, 0)),
                  pl.BlockSpec((1, d), lambda b, r: (0, 0))],
        out_specs=pl.BlockSpec((rows, d), lambda b, r: (b * tiles + r, 0)),
        compiler_params=_params(("parallel", "parallel")),
        name="embed_norm",
    )(x, meta_tokens, g.reshape(1, d))


def _final_norm_kernel(x_ref, g_ref, o_ref):
    x = x_ref[...]
    y = x * lax.rsqrt(jnp.mean(x * x, axis=-1, keepdims=True) + EPS)
    o_ref[...] = y * g_ref[...]


def _final_norm(h, g, *, batch, rows_per_batch, seq, rows):
    d = h.shape[1]
    lead = rows_per_batch - seq
    return pl.pallas_call(
        _final_norm_kernel,
        out_shape=jax.ShapeDtypeStruct((batch, seq, d), F32),
        grid=(batch, seq // rows),
        in_specs=[pl.BlockSpec((pl.Element(1), pl.Element(rows), pl.Element(d)),
                               lambda b, j: (b, pl.multiple_of(lead + rows * j, N_META), 0)),
                  pl.BlockSpec((1, d), lambda b, j: (0, 0))],
        out_specs=pl.BlockSpec((1, rows, d), lambda b, j: (b, j, 0)),
        compiler_params=_params(("parallel", "parallel")),
        name="final_norm",
    )(h.reshape(batch, rows_per_batch, d), g.reshape(1, d))


def _mm_res_kernel(a_ref, w_ref, r_ref, o_ref):
    o_ref[...] = r_ref[...] + _dot(a_ref[...], w_ref[...])


def _matmul_res(a, w, residual, *, tm, tn, name):
    m, k = a.shape
    n = w.shape[1]
    return pl.pallas_call(
        _mm_res_kernel,
        out_shape=jax.ShapeDtypeStruct((m, n), F32),
        grid=(m // tm, n // tn),
        in_specs=[pl.BlockSpec((tm, k), lambda i, j: (i, 0)),
                  pl.BlockSpec((k, tn), lambda i, j: (0, j)),
                  pl.BlockSpec((tm, tn), lambda i, j: (i, j))],
        out_specs=pl.BlockSpec((tm, tn), lambda i, j: (i, j)),
        compiler_params=_params(("parallel", "parallel")),
        name=name,
    )(a, w, residual)


def _out_proj_kernel(a_ref, w_ref, x_ref, meta_ref, o_ref):
    w = w_ref[...].astype(BF16)
    seq = x_ref.shape[1]
    acc = _dot(a_ref[0:N_META + BIG_CHUNK, :], w)
    o_ref[0:N_META, :] = meta_ref[...] + acc[0:N_META]
    o_ref[N_META:N_META + BIG_CHUNK, :] = x_ref[0, 0:BIG_CHUNK, :] + acc[N_META:]
    for lo in range(BIG_CHUNK, seq, BIG_CHUNK):
        rows = slice(N_META + lo, N_META + lo + BIG_CHUNK)
        o_ref[rows, :] = x_ref[0, lo:lo + BIG_CHUNK, :] + _dot(a_ref[rows, :], w)


def _out_proj(a, w, x, meta_tokens, *, tn):
    batch, seq, d = x.shape
    m, k = a.shape
    rows = N_META + seq
    return pl.pallas_call(
        _out_proj_kernel,
        out_shape=jax.ShapeDtypeStruct((m, d), F32),
        grid=(batch, d // tn),
        in_specs=[pl.BlockSpec((rows, k), lambda b, j: (b, 0)),
                  pl.BlockSpec((k, tn), lambda b, j: (0, j)),
                  pl.BlockSpec((1, seq, tn), lambda b, j: (b, 0, j)),
                  pl.BlockSpec((N_META, tn), lambda b, j: (0, j))],
        out_specs=pl.BlockSpec((rows, tn), lambda b, j: (b, j)),
        compiler_params=_params(("parallel", "parallel")),
        name="out_proj",
    )(a, w, x, meta_tokens)


def _in_proj_kernel(a_ref, wt_ref, wz_ref, o_ref, z_ref):
    o_ref[...] = _dot_nt(a_ref[...], wt_ref[...].astype(BF16)).astype(o_ref.dtype)

    @pl.when(pl.program_id(1) == 0)
    def _():
        z_ref[...] = _dot_nt(a_ref[...], wz_ref[...].astype(BF16))


def _in_proj(a, w_in_t, *, tm, tn):
    m, k = a.shape

    def row_start(j):
        return pl.multiple_of(j * tn + jnp.where(j * tn >= N_MAIN, GLA_RANK, 0), GLA_RANK)

    return pl.pallas_call(
        _in_proj_kernel,
        out_shape=(jax.ShapeDtypeStruct((m, N_PROJ), BF16), jax.ShapeDtypeStruct((m, GLA_RANK), F32)),
        grid=(m // tm, N_PROJ // tn),
        in_specs=[pl.BlockSpec((tm, k), lambda i, j: (i, 0), pipeline_mode=pl.Buffered(1)),
                  pl.BlockSpec((pl.Element(tn), pl.Element(k)), lambda i, j: (row_start(j), 0)),
                  pl.BlockSpec((GLA_RANK, k), lambda i, j: (N_MAIN // GLA_RANK, 0))],
        out_specs=(pl.BlockSpec((tm, tn), lambda i, j: (i, j)),
                   pl.BlockSpec((tm, GLA_RANK), lambda i, j: (i, 0))),
        compiler_params=_params(("parallel", "arbitrary")),
        name="in_proj",
    )(a, w_in_t, w_in_t)


def _rope_table_kernel(pos_ref, invf_ref, cos_ref, sin_ref):
    ang = pos_ref[...].astype(F32) * invf_ref[...]
    cos_ref[...] = jnp.cos(ang)
    sin_ref[...] = jnp.sin(ang)


def _rope_tables(pos_col, inv_freq, *, rows):
    m = pos_col.shape[0]
    half = inv_freq.shape[1]
    out = jax.ShapeDtypeStruct((m, half), F32)
    return pl.pallas_call(
        _rope_table_kernel,
        out_shape=(out, out),
        grid=(m // rows,),
        in_specs=[pl.BlockSpec((rows, 1), lambda i: (i, 0)),
                  pl.BlockSpec((1, half), lambda i: (0, 0))],
        out_specs=(pl.BlockSpec((rows, half), lambda i: (i, 0)),
                   pl.BlockSpec((rows, half), lambda i: (i, 0))),
        compiler_params=_params(("parallel",)),
        name="rope_tables",
    )(pos_col, inv_freq)


def _lower_tri(n):
    return (_iota((n, n), 0) >= _iota((n, n), 1)).astype(BF16)


def _gla_log_decay(rows, tri, z_ref, wgu_ref, bg_ref):
    u = _dot(z_ref[rows, :].astype(BF16), wgu_ref[...]) + bg_ref[...]
    log_a = (jnp.minimum(u, 0.0) - jnp.log(1.0 + jnp.exp(-jnp.abs(u)))) * (1.0 / GLA_GATE_TAU)
    hi = log_a.astype(BF16)
    r1 = log_a - hi.astype(F32)
    mid = r1.astype(BF16)
    lo = (r1 - mid.astype(F32)).astype(BF16)
    return _dot(tri, hi) + _dot(tri, mid) + _dot(tri, lo)


def _gla_finish(rows, y, q, k, v, b, g_ref, out_ref, s_ref):
    n = b.shape[0]
    b_last = b[n - 1:n, :]
    kd = k * jnp.exp(b_last - b)
    upd = _dot_tn(kd.astype(BF16), v)
    decay_col = jnp.broadcast_to(jnp.exp(b_last), (LANES, GLA_DK)).T
    s_ref[...] = s_ref[...] * jnp.tile(decay_col, (1, GLA_DV // LANES)) + upd
    yn = y * lax.rsqrt(jnp.mean(y * y, axis=-1, keepdims=True) + EPS) * g_ref[...]
    out_ref[rows, :] = yn.astype(out_ref.dtype)


def _gla_chunk_direct(r0, n, b, q_ref, k_ref, v_ref, g_ref, out_ref, s_ref):
    rows = pl.ds(r0, n)
    q = q_ref[rows, :].astype(F32) * (GLA_DK ** -0.5)
    k = k_ref[rows, :].astype(F32)
    v = v_ref[rows, :]
    qe = (q * jnp.exp(b)).astype(BF16)
    cross = _dot(qe, s_ref[...].astype(BF16))
    scores = _dot_nt(qe, (k * jnp.exp(-b)).astype(BF16))
    causal = _iota((n, n), 0) >= _iota((n, n), 1)
    y = _dot(jnp.where(causal, scores, 0.0).astype(BF16), v) + cross
    _gla_finish(rows, y, q, k, v, b, g_ref, out_ref, s_ref)


def _gla_chunk_exact(r0, c_len, q_ref, k_ref, v_ref, z_ref, wgu_ref, bg_ref, g_ref, out_ref, s_ref):
    sub = GLA_SUB
    rows = pl.ds(r0, c_len)
    b = _gla_log_decay(rows, _lower_tri(c_len), z_ref, wgu_ref, bg_ref)
    q = q_ref[rows, :].astype(F32) * (GLA_DK ** -0.5)
    k = k_ref[rows, :].astype(F32)
    v = v_ref[rows, :]
    cross = _dot((q * jnp.exp(b)).astype(BF16), s_ref[...].astype(BF16))

    key_row = _iota((c_len, 1), 0)
    lane = _iota((sub, c_len), 1)
    qrow = _iota((sub, c_len), 0)
    blocks = []
    for s in range(c_len // sub):
        sl = slice(sub * s, sub * (s + 1))
        bs, qs, ks = b[sl], q[sl], k[sl]
        acc = jnp.zeros((sub, c_len), F32)
        for j in range(sub):
            w = jnp.exp(jnp.minimum(bs - bs[j:j + 1, :], 0.0))
            col = jnp.sum(qs * w * ks[j:j + 1, :], axis=-1, keepdims=True)
            acc = jnp.where((lane == sub * s + j) & (qrow >= j), col, acc)
        if s > 0:
            b_ref_row = b[sub * s - 1:sub * s, :]
            qt = qs * jnp.exp(bs - b_ref_row)
            earlier = key_row < sub * s
            kt = jnp.where(earlier, k * jnp.exp(jnp.where(earlier, b_ref_row - b, 0.0)), 0.0)
            acc = acc + _dot_nt(qt.astype(BF16), kt.astype(BF16))
        blocks.append(acc)
    scores = jnp.concatenate(blocks, axis=0)
    y = _dot(scores.astype(BF16), v) + cross
    _gla_finish(rows, y, q, k, v, b, g_ref, out_ref, s_ref)


def _chunk_span(b):
    n = b.shape[0]
    return jnp.max(-b[n - 1:n, :], axis=1, keepdims=True)


def _gla_chunk_any(r0, n, q_ref, k_ref, v_ref, z_ref, wgu_ref, bg_ref, g_ref, out_ref, s_ref, b_ref):
    b = b_ref[pl.ds(r0, n), :]
    small = _chunk_span(b)[0, 0] <= GLA_SAFE_SPAN

    @pl.when(small)
    def _():
        _gla_chunk_direct(r0, n, b, q_ref, k_ref, v_ref, g_ref, out_ref, s_ref)

    @pl.when(jnp.logical_not(small))
    def _():
        c_len = min(n, GLA_EXACT_CHUNK)

        def body(i, carry):
            _gla_chunk_exact(pl.multiple_of(r0 + i * c_len, GLA_SUB), c_len,
                             q_ref, k_ref, v_ref, z_ref, wgu_ref, bg_ref, g_ref, out_ref, s_ref)
            return carry

        lax.fori_loop(0, n // c_len, body, 0)


def _gla_kernel(q_ref, k_ref, v_ref, z_ref, wgu_ref, bg_ref, g_ref, out_ref, s_ref, b_ref):
    refs = (q_ref, k_ref, v_ref, z_ref, wgu_ref, bg_ref, g_ref, out_ref, s_ref)
    n_big = (q_ref.shape[0] - N_META) // BIG_CHUNK

    def big_start(c):
        return pl.multiple_of(N_META + c * BIG_CHUNK, N_META)

    b_meta = _gla_log_decay(pl.ds(0, N_META), _lower_tri(N_META), z_ref, wgu_ref, bg_ref)
    b_ref[0:N_META, :] = b_meta
    tri = _lower_tri(BIG_CHUNK)

    def decay_body(c, span):
        rows = pl.ds(big_start(c), BIG_CHUNK)
        b = _gla_log_decay(rows, tri, z_ref, wgu_ref, bg_ref)
        b_ref[rows, :] = b
        return jnp.maximum(span, _chunk_span(b))

    span = lax.fori_loop(0, n_big, decay_body, _chunk_span(b_meta), unroll=True)
    all_small = span[0, 0] <= GLA_SAFE_SPAN

    s_ref[...] = jnp.zeros_like(s_ref)

    @pl.when(all_small)
    def _():
        direct_refs = (q_ref, k_ref, v_ref, g_ref, out_ref, s_ref)
        _gla_chunk_direct(0, N_META, b_ref[0:N_META, :], *direct_refs)

        def body(c, carry):
            r0 = big_start(c)
            _gla_chunk_direct(r0, BIG_CHUNK, b_ref[pl.ds(r0, BIG_CHUNK), :], *direct_refs)
            return carry

        lax.fori_loop(0, n_big, body, 0)

    @pl.when(jnp.logical_not(all_small))
    def _():
        _gla_chunk_any(0, N_META, *refs, b_ref)

        def body(c, carry):
            _gla_chunk_any(big_start(c), BIG_CHUNK, *refs, b_ref)
            return carry

        lax.fori_loop(0, n_big, body, 0)


def _gla(proj, z, wgu, bg, g_gla, *, batch, rows):
    m = proj.shape[0]
    return pl.pallas_call(
        _gla_kernel,
        out_shape=jax.ShapeDtypeStruct((m, D_MODEL), BF16),
        grid=(batch, GLA_HEADS),
        in_specs=[
            pl.BlockSpec((rows, GLA_DK), lambda b, h: (b, COL_Q_G // GLA_DK + h)),
            pl.BlockSpec((rows, GLA_DK), lambda b, h: (b, COL_K_G // GLA_DK + h)),
            pl.BlockSpec((rows, GLA_DV), lambda b, h: (b, COL_V_G // GLA_DV + h)),
            pl.BlockSpec((rows, GLA_RANK), lambda b, h: (b, 0)),
            pl.BlockSpec((GLA_RANK, GLA_DK), lambda b, h: (0, h)),
            pl.BlockSpec((1, GLA_DK), lambda b, h: (0, h)),
            pl.BlockSpec((1, GLA_DV), lambda b, h: (0, h)),
        ],
        out_specs=pl.BlockSpec((rows, GLA_DV), lambda b, h: (b, h)),
        scratch_shapes=[pltpu.VMEM((GLA_DK, GLA_DV), F32), pltpu.VMEM((rows, GLA_DK), F32)],
        compiler_params=_params(("parallel", "parallel")),
        name="gla",
    )(proj, proj, proj, z, wgu, bg, g_gla.reshape(1, D_MODEL))


def _ret_decays(lg, n):
    rel = (_iota((n, n), 0) - _iota((n, n), 1)).astype(F32)
    d_intra = jnp.where(rel >= 0, jnp.exp(lg * jnp.maximum(rel, 0.0)), 0.0)
    ridx = _iota((n, 1), 0).astype(F32)
    d_q = jnp.exp(lg * (ridx + 1.0))
    d_k = jnp.exp(lg * (n - 1.0 - ridx))
    d_chunk = jnp.exp(lg * float(n))
    return d_intra, d_q, d_k, d_chunk


def _ret_chunk(r0, decays, q_ref, k_ref, v_ref, o_ref, m_ref, cos_ref, sin_ref, g_ref,
               yg_ref, og_ref, mg_ref, out_ref, s_ref):
    d_intra, d_q, d_k, d_chunk = decays
    n = d_intra.shape[0]
    rows = pl.ds(r0, n)
    half = RET_DK // 2
    cos = cos_ref[rows, :]
    sin = sin_ref[rows, :]

    def rope(ref):
        x = ref[rows, :].astype(F32)
        x1, x2 = x[:, :half], x[:, half:]
        return jnp.concatenate([x1 * cos - x2 * sin, x2 * cos + x1 * sin], axis=-1)

    q = rope(q_ref)
    k = rope(k_ref) * (RET_DK ** -0.5)
    v = v_ref[rows, :]

    qb = q.astype(BF16)
    scores = _dot_nt(qb, k.astype(BF16)) * d_intra
    state = s_ref[...]
    y = _dot(scores.astype(BF16), v) + _dot(qb, state.astype(BF16)) * d_q
    s_ref[...] = state * d_chunk + _dot_tn((k * d_k).astype(BF16), v)

    mu = jnp.mean(y, axis=-1, keepdims=True)
    yc = y - mu
    var = jnp.mean(yc * yc, axis=-1, keepdims=True)
    yn = yc * lax.rsqrt(var + EPS) * g_ref[...]

    def gated4(branch, o_gate_ref, m_gate_ref):
        o = o_gate_ref[rows, :]
        m = m_gate_ref[rows, :]
        return (1.0 + jnp.tanh(0.5 * m)) * (1.0 + jnp.tanh(0.5 * o)) * (o * branch)

    merged = 0.25 * (gated4(yn.astype(BF16), o_ref, m_ref) + gated4(yg_ref[rows, :], og_ref, mg_ref))
    out_ref[rows, :] = merged.astype(out_ref.dtype)


def _ret_kernel(lg_ref, q_ref, k_ref, v_ref, o_ref, m_ref, cos_ref, sin_ref, g_ref,
                yg_ref, og_ref, mg_ref, out_ref, s_ref):
    lg = lg_ref[0][:, :1]
    refs = (q_ref, k_ref, v_ref, o_ref, m_ref, cos_ref, sin_ref, g_ref, yg_ref, og_ref, mg_ref,
            out_ref, s_ref)
    s_ref[...] = jnp.zeros_like(s_ref)
    _ret_chunk(0, _ret_decays(lg, N_META), *refs)
    decays = _ret_decays(lg, BIG_CHUNK)

    def body(c, carry):
        r0 = pl.multiple_of(N_META + c * BIG_CHUNK, N_META)
        _ret_chunk(r0, decays, *refs)
        return carry

    lax.fori_loop(0, (q_ref.shape[0] - N_META) // BIG_CHUNK, body, 0, unroll=True)


def _retention_merge(proj, y_gla, cos, sin, g_ret, *, batch, rows):
    m = proj.shape[0]
    log_gamma = jnp.log1p(-jnp.exp2(-5.0 - jnp.arange(RET_HEADS, dtype=F32)))
    lg = jnp.broadcast_to(log_gamma[:, None, None], (RET_HEADS, 1, LANES))

    def cols(start, width):
        return lambda b, h: (b, start // width + h)

    dk, dv = RET_DK, RET_DV
    return pl.pallas_call(
        _ret_kernel,
        out_shape=jax.ShapeDtypeStruct((m, D_MODEL), BF16),
        grid=(batch, RET_HEADS),
        in_specs=[
            pl.BlockSpec((1, 1, LANES), lambda b, h: (h, 0, 0)),
            pl.BlockSpec((rows, dk), cols(COL_Q_R, dk)),
            pl.BlockSpec((rows, dk), cols(COL_K_R, dk)),
            pl.BlockSpec((rows, dv), cols(COL_V_R, dv)),
            pl.BlockSpec((rows, dv), cols(COL_O_R, dv)),
            pl.BlockSpec((rows, dv), cols(COL_M_R, dv)),
            pl.BlockSpec((rows, dk // 2), lambda b, h: (b, 0)),
            pl.BlockSpec((rows, dk // 2), lambda b, h: (b, 0)),
            pl.BlockSpec((1, dv), lambda b, h: (0, h)),
            pl.BlockSpec((rows, dv), lambda b, h: (b, h)),
            pl.BlockSpec((rows, dv), cols(COL_O_G, dv)),
            pl.BlockSpec((rows, dv), cols(COL_M_G, dv)),
        ],
        out_specs=pl.BlockSpec((rows, dv), lambda b, h: (b, h)),
        scratch_shapes=[pltpu.VMEM((dk, dv), F32)],
        compiler_params=_params(("parallel", "parallel")),
        name="retention_merge",
    )(lg, proj, proj, proj, proj, proj, cos, sin, g_ret.reshape(1, D_MODEL), y_gla, proj, proj)


def _ffn_in_kernel(a_ref, wu_ref, wg_ref, cw_ref, cb_ref, wo_ref, hid_ref, wo_bf16_ref):
    wo_bf16_ref[...] = wo_ref[...].astype(BF16)
    rows = a_ref.shape[0]
    tf = wu_ref.shape[1]
    w = jnp.concatenate([wu_ref[...].astype(BF16), wg_ref[...].astype(BF16)], axis=1)
    cw = 0.5 * cw_ref[...]
    cb = 0.5 * cb_ref[...]
    tail = jnp.zeros((8, tf), F32)
    starts = list(range(0, rows - FFN_ROW_CHUNK + 1, FFN_ROW_CHUNK))
    for lo, hi in zip(starts, starts[1:] + [rows]):
        up_gate = _dot(a_ref[lo:hi, :], w)
        up, gate = up_gate[:, :tf], up_gate[:, tf:]
        ext = jnp.concatenate([tail, up], axis=0)
        up_m1 = pltpu.roll(ext, 1, axis=0)[8:]
        up_m2 = pltpu.roll(ext, 2, axis=0)[8:]
        ch = cb + cw[0:1, :] * up_m2 + cw[1:2, :] * up_m1 + cw[2:3, :] * up
        hid_ref[lo:hi, :] = (ch * (1.0 + jnp.tanh(ch)) * gate).astype(hid_ref.dtype)
        tail = up[hi - lo - 8:]


def _ffn_in(a, w_ffn_in, conv_w, conv_b, w_ffn_out, *, rows, tf):
    m, k = a.shape
    nf = D_FF // tf
    n_steps = (m // rows) * nf
    d_out = w_ffn_out.shape[1]
    slab = D_FF // n_steps
    assert slab * n_steps == D_FF and slab % 16 == 0
    return pl.pallas_call(
        _ffn_in_kernel,
        out_shape=(jax.ShapeDtypeStruct((m, D_FF), BF16),
                   jax.ShapeDtypeStruct((D_FF, d_out), BF16)),
        grid=(m // rows, nf),
        in_specs=[
            pl.BlockSpec((rows, k), lambda i, j: (i, 0), pipeline_mode=pl.Buffered(1)),
            pl.BlockSpec((k, tf), lambda i, j: (0, j)),
            pl.BlockSpec((k, tf), lambda i, j: (0, nf + j)),
            pl.BlockSpec((CONV_W, tf), lambda i, j: (0, j)),
            pl.BlockSpec((1, tf), lambda i, j: (0, j)),
            pl.BlockSpec((slab, d_out), lambda i, j: (i * nf + j, 0)),
        ],
        out_specs=(pl.BlockSpec((rows, tf), lambda i, j: (i, j)),
                   pl.BlockSpec((slab, d_out), lambda i, j: (i * nf + j, 0))),
        compiler_params=_params(("parallel", "parallel")),
        name="ffn_in_conv",
    )(a, w_ffn_in, w_ffn_in, conv_w, conv_b.reshape(1, D_FF), w_ffn_out)


def kernel(x, positions, meta_tokens, attn_norm, w_in, w_gate_up, b_gate, ret_norm, gla_norm,
           w_out, ffn_norm, w_ffn_in, conv_w, conv_b, w_ffn_out, final_norm):
    batch, seq, d = x.shape
    rows = N_META + seq
    m = batch * rows

    meta_tokens = meta_tokens.astype(x.dtype)
    pos = jnp.concatenate([
        jnp.broadcast_to(jnp.arange(N_META, dtype=jnp.int32), (batch, N_META)),
        positions.astype(jnp.int32) + N_META], axis=1).reshape(m, 1)
    half = RET_DK // 2
    inv_freq = (ROPE_BASE ** (-jnp.arange(half, dtype=F32) / half)).reshape(1, half)

    w_in_t = jnp.swapaxes(w_in, 1, 2)[0]

    cos, sin = _rope_tables(pos, inv_freq, rows=rows)

    norm_rows = rows // 3
    hn = _embed_norm(x, meta_tokens, attn_norm[0], rows=norm_rows, out_dtype=BF16)
    proj, z = _in_proj(hn, w_in_t, tm=rows, tn=512)

    y_gla = _gla(proj, z, w_gate_up[0].astype(BF16), b_gate[0].reshape(1, -1), gla_norm[0],
                 batch=batch, rows=rows)
    merged = _retention_merge(proj, y_gla, cos, sin, ret_norm[0], batch=batch, rows=rows)

    h1 = _out_proj(merged, w_out[0], x, meta_tokens, tn=256)
    h1n = _rmsnorm(h1, ffn_norm[0], rows=norm_rows, out_dtype=BF16)
    hidden, w_ffn_out_b = _ffn_in(h1n, w_ffn_in[0], conv_w[0], conv_b[0], w_ffn_out[0],
                                  rows=rows, tf=256)
    h2 = _matmul_res(hidden, w_ffn_out_b, h1, tm=norm_rows, tn=256, name="ffn_out")
    return _final_norm(h2, final_norm, batch=batch, rows_per_batch=rows, seq=seq, rows=512)
```

```python
import jax
import jax.numpy as jnp
from jax import lax
from jax.experimental import pallas as pl
from jax.experimental.pallas import tpu as pltpu

F32 = jnp.float32
BF16 = jnp.bfloat16

D_MODEL = 4096
N_META = 16
RET_HEADS, RET_DK, RET_DV = 8, 256, 512
GLA_HEADS, GLA_DK, GLA_DV = 4, 512, 1024
GLA_RANK = 16
GLA_GATE_TAU = 16.0
D_FF = 11008
CONV_W = 3
ROPE_BASE = 10000.0
EPS = 1e-6
RET_QK = RET_HEADS * RET_DK
GLA_QK = GLA_HEADS * GLA_DK
COL_Q_R, COL_K_R = 0, RET_QK
COL_V_R, COL_O_R = 2 * RET_QK, 2 * RET_QK + D_MODEL
COL_Q_G = 2 * RET_QK + 2 * D_MODEL
COL_K_G, COL_V_G = COL_Q_G + GLA_QK, COL_Q_G + 2 * GLA_QK
COL_O_G = COL_V_G + D_MODEL
N_MAIN = COL_O_G + D_MODEL
COL_M_R, COL_M_G = N_MAIN, N_MAIN + D_MODEL
N_PROJ = N_MAIN + 2 * D_MODEL

VMEM_LIMIT_V7X = 56 * 1024 * 1024
LANES = 128
BIG_CHUNK = 256
FFN_ROW_CHUNK = 128
GLA_EXACT_CHUNK = 64
GLA_SUB = 16
GLA_SAFE_SPAN = 60.0


def _params(sem):
    return pltpu.CompilerParams(dimension_semantics=sem, vmem_limit_bytes=VMEM_LIMIT_V7X)


def _iota(shape, dim):
    return lax.broadcasted_iota(jnp.int32, shape, dim)


def _dot(a, b):
    return jnp.dot(a, b, preferred_element_type=F32)


def _dot_nt(a, b):
    return lax.dot_general(a, b, (((1,), (1,)), ((), ())), preferred_element_type=F32)


def _dot_tn(a, b):
    return lax.dot_general(a, b, (((0,), (0,)), ((), ())), preferred_element_type=F32)


def _rmsnorm_kernel(x_ref, g_ref, o_ref):
    x = x_ref[...]
    y = x * lax.rsqrt(jnp.mean(x * x, axis=-1, keepdims=True) + EPS)
    o_ref[...] = (y * g_ref[...]).astype(o_ref.dtype)


def _rmsnorm(x, g, *, rows, out_dtype):
    m, d = x.shape
    return pl.pallas_call(
        _rmsnorm_kernel,
        out_shape=jax.ShapeDtypeStruct((m, d), out_dtype),
        grid=(m // rows,),
        in_specs=[pl.BlockSpec((rows, d), lambda i: (i, 0)),
                  pl.BlockSpec((1, d), lambda i: (0, 0))],
        out_specs=pl.BlockSpec((rows, d), lambda i: (i, 0)),
        compiler_params=_params(("parallel",)),
        name="rmsnorm",
    )(x, g.reshape(1, d))


def _embed_norm_kernel(x_ref, meta_ref, g_ref, o_ref):
    def norm(v):
        y = v * lax.rsqrt(jnp.mean(v * v, axis=-1, keepdims=True) + EPS)
        return (y * g_ref[...]).astype(o_ref.dtype)

    r = pl.program_id(1)
    rows = o_ref.shape[0]

    @pl.when(r == 0)
    def _():
        o_ref[0:N_META, :] = norm(meta_ref[...])
        o_ref[N_META:, :] = norm(x_ref[0, 0:rows - N_META, :])

    @pl.when(r > 0)
    def _():
        o_ref[...] = norm(x_ref[0])


def _embed_norm(x, meta_tokens, g, *, rows, out_dtype):
    batch, seq, d = x.shape
    tiles = (N_META + seq) // rows

    def x_start(r):
        return pl.multiple_of(jnp.maximum(rows * r - N_META, 0), N_META)

    return pl.pallas_call(
        _embed_norm_kernel,
        out_shape=jax.ShapeDtypeStruct((batch * (N_META + seq), d), out_dtype),
        grid=(batch, tiles),
        in_specs=[pl.BlockSpec((pl.Element(1), pl.Element(rows), pl.Element(d)),
                               lambda b, r: (b, x_start(r), 0)),
                  pl.BlockSpec((N_META, d), lambda b, r: (0, 0)),
                  pl.BlockSpec((1, d), lambda b, r: (0, 0))],
        out_specs=pl.BlockSpec((rows, d), lambda b, r: (b * tiles + r, 0)),
        compiler_params=_params(("parallel", "parallel")),
        name="embed_norm",
    )(x, meta_tokens, g.reshape(1, d))


def _final_norm_kernel(x_ref, g_ref, o_ref):
    x = x_ref[...]
    y = x * lax.rsqrt(jnp.mean(x * x, axis=-1, keepdims=True) + EPS)
    o_ref[...] = y * g_ref[...]


def _final_norm(h, g, *, batch, rows_per_batch, seq, rows):
    d = h.shape[1]
    lead = rows_per_batch - seq
    return pl.pallas_call(
        _final_norm_kernel,
        out_shape=jax.ShapeDtypeStruct((batch, seq, d), F32),
        grid=(batch, seq // rows),
        in_specs=[pl.BlockSpec((pl.Element(1), pl.Element(rows), pl.Element(d)),
                               lambda b, j: (b, pl.multiple_of(lead + rows * j, N_META), 0)),
                  pl.BlockSpec((1, d), lambda b, j: (0, 0))],
        out_specs=pl.BlockSpec((1, rows, d), lambda b, j: (b, j, 0)),
        compiler_params=_params(("parallel", "parallel")),
        name="final_norm",
    )(h.reshape(batch, rows_per_batch, d), g.reshape(1, d))


def _mm_res_kernel(a_ref, w_ref, r_ref, o_ref):
    o_ref[...] = r_ref[...] + _dot(a_ref[...], w_ref[...])


def _matmul_res(a, w, residual, *, tm, tn, name):
    m, k = a.shape
    n = w.shape[1]
    return pl.pallas_call(
        _mm_res_kernel,
        out_shape=jax.ShapeDtypeStruct((m, n), F32),
        grid=(m // tm, n // tn),
        in_specs=[pl.BlockSpec((tm, k), lambda i, j: (i, 0)),
                  pl.BlockSpec((k, tn), lambda i, j: (0, j)),
                  pl.BlockSpec((tm, tn), lambda i, j: (i, j))],
        out_specs=pl.BlockSpec((tm, tn), lambda i, j: (i, j)),
        compiler_params=_params(("parallel", "parallel")),
        name=name,
    )(a, w, residual)


def _out_proj_kernel(a_ref, w_ref, x_ref, meta_ref, o_ref):
    w = w_ref[...].astype(BF16)
    seq = x_ref.shape[1]
    acc = _dot(a_ref[0:N_META + BIG_CHUNK, :], w)
    o_ref[0:N_META, :] = meta_ref[...] + acc[0:N_META]
    o_ref[N_META:N_META + BIG_CHUNK, :] = x_ref[0, 0:BIG_CHUNK, :] + acc[N_META:]
    for lo in range(BIG_CHUNK, seq, BIG_CHUNK):
        rows = slice(N_META + lo, N_META + lo + BIG_CHUNK)
        o_ref[rows, :] = x_ref[0, lo:lo + BIG_CHUNK, :] + _dot(a_ref[rows, :], w)


def _out_proj(a, w, x, meta_tokens, *, tn):
    batch, seq, d = x.shape
    m, k = a.shape
    rows = N_META + seq
    return pl.pallas_call(
        _out_proj_kernel,
        out_shape=jax.ShapeDtypeStruct((m, d), F32),
        grid=(batch, d // tn),
        in_specs=[pl.BlockSpec((rows, k), lambda b, j: (b, 0)),
                  pl.BlockSpec((k, tn), lambda b, j: (0, j)),
                  pl.BlockSpec((1, seq, tn), lambda b, j: (b, 0, j)),
                  pl.BlockSpec((N_META, tn), lambda b, j: (0, j))],
        out_specs=pl.BlockSpec((rows, tn), lambda b, j: (b, j)),
        compiler_params=_params(("parallel", "parallel")),
        name="out_proj",
    )(a, w, x, meta_tokens)


def _in_proj_kernel(a_ref, wt_ref, wz_ref, o_ref, z_ref):
    o_ref[...] = _dot_nt(a_ref[...], wt_ref[...].astype(BF16)).astype(o_ref.dtype)

    @pl.when(pl.program_id(1) == 0)
    def _():
        z_ref[...] = _dot_nt(a_ref[...], wz_ref[...].astype(BF16))


def _in_proj(a, w_in_t, *, tm, tn):
    m, k = a.shape

    def row_start(j):
        return pl.multiple_of(j * tn + jnp.where(j * tn >= N_MAIN, GLA_RANK, 0), GLA_RANK)

    return pl.pallas_call(
        _in_proj_kernel,
        out_shape=(jax.ShapeDtypeStruct((m, N_PROJ), BF16), jax.ShapeDtypeStruct((m, GLA_RANK), F32)),
        grid=(m // tm, N_PROJ // tn),
        in_specs=[pl.BlockSpec((tm, k), lambda i, j: (i, 0), pipeline_mode=pl.Buffered(1)),
                  pl.BlockSpec((pl.Element(tn), pl.Element(k)), lambda i, j: (row_start(j), 0)),
                  pl.BlockSpec((GLA_RANK, k), lambda i, j: (N_MAIN // GLA_RANK, 0))],
        out_specs=(pl.BlockSpec((tm, tn), lambda i, j: (i, j)),
                   pl.BlockSpec((tm, GLA_RANK), lambda i, j: (i, 0))),
        compiler_params=_params(("parallel", "arbitrary")),
        name="in_proj",
    )(a, w_in_t, w_in_t)


def _rope_table_kernel(pos_ref, invf_ref, cos_ref, sin_ref):
    ang = pos_ref[...].astype(F32) * invf_ref[...]
    cos_ref[...] = jnp.cos(ang)
    sin_ref[...] = jnp.sin(ang)


def _rope_tables(pos_col, inv_freq, *, rows):
    m = pos_col.shape[0]
    half = inv_freq.shape[1]
    out = jax.ShapeDtypeStruct((m, half), F32)
    return pl.pallas_call(
        _rope_table_kernel,
        out_shape=(out, out),
        grid=(m // rows,),
        in_specs=[pl.BlockSpec((rows, 1), lambda i: (i, 0)),
                  pl.BlockSpec((1, half), lambda i: (0, 0))],
        out_specs=(pl.BlockSpec((rows, half), lambda i: (i, 0)),
                   pl.BlockSpec((rows, half), lambda i: (i, 0))),
        compiler_params=_params(("parallel",)),
        name="rope_tables",
    )(pos_col, inv_freq)


def _lower_tri(n):
    return (_iota((n, n), 0) >= _iota((n, n), 1)).astype(BF16)


def _gla_log_decay(rows, tri, z_ref, wgu_ref, bg_ref):
    u = _dot(z_ref[rows, :].astype(BF16), wgu_ref[...]) + bg_ref[...]
    log_a = (jnp.minimum(u, 0.0) - jnp.log(1.0 + jnp.exp(-jnp.abs(u)))) * (1.0 / GLA_GATE_TAU)
    hi = log_a.astype(BF16)
    r1 = log_a - hi.astype(F32)
    mid = r1.astype(BF16)
    lo = (r1 - mid.astype(F32)).astype(BF16)
    return _dot(tri, hi) + _dot(tri, mid) + _dot(tri, lo)


def _gla_finish(rows, y, q, k, v, b, g_ref, out_ref, s_ref):
    n = b.shape[0]
    b_last = b[n - 1:n, :]
    kd = k * jnp.exp(b_last - b)
    upd = _dot_tn(kd.astype(BF16), v)
    decay_col = jnp.broadcast_to(jnp.exp(b_last), (LANES, GLA_DK)).T
    s_ref[...] = s_ref[...] * jnp.tile(decay_col, (1, GLA_DV // LANES)) + upd
    yn = y * lax.rsqrt(jnp.mean(y * y, axis=-1, keepdims=True) + EPS) * g_ref[...]
    out_ref[rows, :] = yn.astype(out_ref.dtype)


def _gla_chunk_direct(r0, n, b, q_ref, k_ref, v_ref, g_ref, out_ref, s_ref):
    rows = pl.ds(r0, n)
    q = q_ref[rows, :].astype(F32) * (GLA_DK ** -0.5)
    k = k_ref[rows, :].astype(F32)
    v = v_ref[rows, :]
    qe = (q * jnp.exp(b)).astype(BF16)
    cross = _dot(qe, s_ref[...].astype(BF16))
    scores = _dot_nt(qe, (k * jnp.exp(-b)).astype(BF16))
    causal = _iota((n, n), 0) >= _iota((n, n), 1)
    y = _dot(jnp.where(causal, scores, 0.0).astype(BF16), v) + cross
    _gla_finish(rows, y, q, k, v, b, g_ref, out_ref, s_ref)


def _gla_chunk_exact(r0, c_len, q_ref, k_ref, v_ref, z_ref, wgu_ref, bg_ref, g_ref, out_ref, s_ref):
    sub = GLA_SUB
    rows = pl.ds(r0, c_len)
    b = _gla_log_decay(rows, _lower_tri(c_len), z_ref, wgu_ref, bg_ref)
    q = q_ref[rows, :].astype(F32) * (GLA_DK ** -0.5)
    k = k_ref[rows, :].astype(F32)
    v = v_ref[rows, :]
    cross = _dot((q * jnp.exp(b)).astype(BF16), s_ref[...].astype(BF16))

    key_row = _iota((c_len, 1), 0)
    lane = _iota((sub, c_len), 1)
    qrow = _iota((sub, c_len), 0)
    blocks = []
    for s in range(c_len // sub):
        sl = slice(sub * s, sub * (s + 1))
        bs, qs, ks = b[sl], q[sl], k[sl]
        acc = jnp.zeros((sub, c_len), F32)
        for j in range(sub):
            w = jnp.exp(jnp.minimum(bs - bs[j:j + 1, :], 0.0))
            col = jnp.sum(qs * w * ks[j:j + 1, :], axis=-1, keepdims=True)
            acc = jnp.where((lane == sub * s + j) & (qrow >= j), col, acc)
        if s > 0:
            b_ref_row = b[sub * s - 1:sub * s, :]
            qt = qs * jnp.exp(bs - b_ref_row)
            earlier = key_row < sub * s
            kt = jnp.where(earlier, k * jnp.exp(jnp.where(earlier, b_ref_row - b, 0.0)), 0.0)
            acc = acc + _dot_nt(qt.astype(BF16), kt.astype(BF16))
        blocks.append(acc)
    scores = jnp.concatenate(blocks, axis=0)
    y = _dot(scores.astype(BF16), v) + cross
    _gla_finish(rows, y, q, k, v, b, g_ref, out_ref, s_ref)


def _chunk_span(b):
    n = b.shape[0]
    return jnp.max(-b[n - 1:n, :], axis=1, keepdims=True)


def _gla_chunk_any(r0, n, q_ref, k_ref, v_ref, z_ref, wgu_ref, bg_ref, g_ref, out_ref, s_ref, b_ref):
    b = b_ref[pl.ds(r0, n), :]
    small = _chunk_span(b)[0, 0] <= GLA_SAFE_SPAN

    @pl.when(small)
    def _():
        _gla_chunk_direct(r0, n, b, q_ref, k_ref, v_ref, g_ref, out_ref, s_ref)

    @pl.when(jnp.logical_not(small))
    def _():
        c_len = min(n, GLA_EXACT_CHUNK)

        def body(i, carry):
            _gla_chunk_exact(pl.multiple_of(r0 + i * c_len, GLA_SUB), c_len,
                             q_ref, k_ref, v_ref, z_ref, wgu_ref, bg_ref, g_ref, out_ref, s_ref)
            return carry

        lax.fori_loop(0, n // c_len, body, 0)


def _gla_kernel(q_ref, k_ref, v_ref, z_ref, wgu_ref, bg_ref, g_ref, out_ref, s_ref, b_ref):
    refs = (q_ref, k_ref, v_ref, z_ref, wgu_ref, bg_ref, g_ref, out_ref, s_ref)
    n_big = (q_ref.shape[0] - N_META) // BIG_CHUNK

    def big_start(c):
        return pl.multiple_of(N_META + c * BIG_CHUNK, N_META)

    b_meta = _gla_log_decay(pl.ds(0, N_META), _lower_tri(N_META), z_ref, wgu_ref, bg_ref)
    b_ref[0:N_META, :] = b_meta
    tri = _lower_tri(BIG_CHUNK)

    def decay_body(c, span):
        rows = pl.ds(big_start(c), BIG_CHUNK)
        b = _gla_log_decay(rows, tri, z_ref, wgu_ref, bg_ref)
        b_ref[rows, :] = b
        return jnp.maximum(span, _chunk_span(b))

    span = lax.fori_loop(0, n_big, decay_body, _chunk_span(b_meta), unroll=True)
    all_small = span[0, 0] <= GLA_SAFE_SPAN

    s_ref[...] = jnp.zeros_like(s_ref)

    @pl.when(all_small)
    def _():
        direct_refs = (q_ref, k_ref, v_ref, g_ref, out_ref, s_ref)
        _gla_chunk_direct(0, N_META, b_ref[0:N_META, :], *direct_refs)

        def body(c, carry):
            r0 = big_start(c)
            _gla_chunk_direct(r0, BIG_CHUNK, b_ref[pl.ds(r0, BIG_CHUNK), :], *direct_refs)
            return carry

        lax.fori_loop(0, n_big, body, 0)

    @pl.when(jnp.logical_not(all_small))
    def _():
        _gla_chunk_any(0, N_META, *refs, b_ref)

        def body(c, carry):
            _gla_chunk_any(big_start(c), BIG_CHUNK, *refs, b_ref)
            return carry

        lax.fori_loop(0, n_big, body, 0)


def _gla(proj, z, wgu, bg, g_gla, *, batch, rows):
    m = proj.shape[0]
    return pl.pallas_call(
        _gla_kernel,
        out_shape=jax.ShapeDtypeStruct((m, D_MODEL), BF16),
        grid=(batch, GLA_HEADS),
        in_specs=[
            pl.BlockSpec((rows, GLA_DK), lambda b, h: (b, COL_Q_G // GLA_DK + h)),
            pl.BlockSpec((rows, GLA_DK), lambda b, h: (b, COL_K_G // GLA_DK + h)),
            pl.BlockSpec((rows, GLA_DV), lambda b, h: (b, COL_V_G // GLA_DV + h)),
            pl.BlockSpec((rows, GLA_RANK), lambda b, h: (b, 0)),
            pl.BlockSpec((GLA_RANK, GLA_DK), lambda b, h: (0, h)),
            pl.BlockSpec((1, GLA_DK), lambda b, h: (0, h)),
            pl.BlockSpec((1, GLA_DV), lambda b, h: (0, h)),
        ],
        out_specs=pl.BlockSpec((rows, GLA_DV), lambda b, h: (b, h)),
        scratch_shapes=[pltpu.VMEM((GLA_DK, GLA_DV), F32), pltpu.VMEM((rows, GLA_DK), F32)],
        compiler_params=_params(("parallel", "parallel")),
        name="gla",
    )(proj, proj, proj, z, wgu, bg, g_gla.reshape(1, D_MODEL))


def _ret_decays(lg, n):
    rel = (_iota((n, n), 0) - _iota((n, n), 1)).astype(F32)
    d_intra = jnp.where(rel >= 0, jnp.exp(lg * jnp.maximum(rel, 0.0)), 0.0)
    ridx = _iota((n, 1), 0).astype(F32)
    d_q = jnp.exp(lg * (ridx + 1.0))
    d_k = jnp.exp(lg * (n - 1.0 - ridx))
    d_chunk = jnp.exp(lg * float(n))
    return d_intra, d_q, d_k, d_chunk


def _ret_chunk(r0, decays, q_ref, k_ref, v_ref, o_ref, m_ref, cos_ref, sin_ref, g_ref,
               yg_ref, og_ref, mg_ref, out_ref, s_ref):
    d_intra, d_q, d_k, d_chunk = decays
    n = d_intra.shape[0]
    rows = pl.ds(r0, n)
    half = RET_DK // 2
    cos = cos_ref[rows, :]
    sin = sin_ref[rows, :]

    def rope(ref):
        x = ref[rows, :].astype(F32)
        x1, x2 = x[:, :half], x[:, half:]
        return jnp.concatenate([x1 * cos - x2 * sin, x2 * cos + x1 * sin], axis=-1)

    q = rope(q_ref)
    k = rope(k_ref) * (RET_DK ** -0.5)
    v = v_ref[rows, :]

    qb = q.astype(BF16)
    scores = _dot_nt(qb, k.astype(BF16)) * d_intra
    state = s_ref[...]
    y = _dot(scores.astype(BF16), v) + _dot(qb, state.astype(BF16)) * d_q
    s_ref[...] = state * d_chunk + _dot_tn((k * d_k).astype(BF16), v)

    mu = jnp.mean(y, axis=-1, keepdims=True)
    yc = y - mu
    var = jnp.mean(yc * yc, axis=-1, keepdims=True)
    yn = yc * lax.rsqrt(var + EPS) * g_ref[...]

    def gated4(branch, o_gate_ref, m_gate_ref):
        o = o_gate_ref[rows, :]
        m = m_gate_ref[rows, :]
        return (1.0 + jnp.tanh(0.5 * m)) * (1.0 + jnp.tanh(0.5 * o)) * (o * branch)

    merged = 0.25 * (gated4(yn.astype(BF16), o_ref, m_ref) + gated4(yg_ref[rows, :], og_ref, mg_ref))
    out_ref[rows, :] = merged.astype(out_ref.dtype)


def _ret_kernel(lg_ref, q_ref, k_ref, v_ref, o_ref, m_ref, cos_ref, sin_ref, g_ref,
                yg_ref, og_ref, mg_ref, out_ref, s_ref):
    lg = lg_ref[0][:, :1]
    refs = (q_ref, k_ref, v_ref, o_ref, m_ref, cos_ref, sin_ref, g_ref, yg_ref, og_ref, mg_ref,
            out_ref, s_ref)
    s_ref[...] = jnp.zeros_like(s_ref)
    _ret_chunk(0, _ret_decays(lg, N_META), *refs)
    decays = _ret_decays(lg, BIG_CHUNK)

    def body(c, carry):
        r0 = pl.multiple_of(N_META + c * BIG_CHUNK, N_META)
        _ret_chunk(r0, decays, *refs)
        return carry

    lax.fori_loop(0, (q_ref.shape[0] - N_META) // BIG_CHUNK, body, 0, unroll=True)


def _retention_merge(proj, y_gla, cos, sin, g_ret, *, batch, rows):
    m = proj.shape[0]
    log_gamma = jnp.log1p(-jnp.exp2(-5.0 - jnp.arange(RET_HEADS, dtype=F32)))
    lg = jnp.broadcast_to(log_gamma[:, None, None], (RET_HEADS, 1, LANES))

    def cols(start, width):
        return lambda b, h: (b, start // width + h)

    dk, dv = RET_DK, RET_DV
    return pl.pallas_call(
        _ret_kernel,
        out_shape=jax.ShapeDtypeStruct((m, D_MODEL), BF16),
        grid=(batch, RET_HEADS),
        in_specs=[
            pl.BlockSpec((1, 1, LANES), lambda b, h: (h, 0, 0)),
            pl.BlockSpec((rows, dk), cols(COL_Q_R, dk)),
            pl.BlockSpec((rows, dk), cols(COL_K_R, dk)),
            pl.BlockSpec((rows, dv), cols(COL_V_R, dv)),
            pl.BlockSpec((rows, dv), cols(COL_O_R, dv)),
            pl.BlockSpec((rows, dv), cols(COL_M_R, dv)),
            pl.BlockSpec((rows, dk // 2), lambda b, h: (b, 0)),
            pl.BlockSpec((rows, dk // 2), lambda b, h: (b, 0)),
            pl.BlockSpec((1, dv), lambda b, h: (0, h)),
            pl.BlockSpec((rows, dv), lambda b, h: (b, h)),
            pl.BlockSpec((rows, dv), cols(COL_O_G, dv)),
            pl.BlockSpec((rows, dv), cols(COL_M_G, dv)),
        ],
        out_specs=pl.BlockSpec((rows, dv), lambda b, h: (b, h)),
        scratch_shapes=[pltpu.VMEM((dk, dv), F32)],
        compiler_params=_params(("parallel", "parallel")),
        name="retention_merge",
    )(lg, proj, proj, proj, proj, proj, cos, sin, g_ret.reshape(1, D_MODEL), y_gla, proj, proj)


def _ffn_in_kernel(a_ref, wu_ref, wg_ref, cw_ref, cb_ref, wo_ref, hid_ref, wo_bf16_ref):
    wo_bf16_ref[...] = wo_ref[...].astype(BF16)
    rows = a_ref.shape[0]
    tf = wu_ref.shape[1]
    w = jnp.concatenate([wu_ref[...].astype(BF16), wg_ref[...].astype(BF16)], axis=1)
    cw = 0.5 * cw_ref[...]
    cb = 0.5 * cb_ref[...]
    tail = jnp.zeros((8, tf), F32)
    starts = list(range(0, rows - FFN_ROW_CHUNK + 1, FFN_ROW_CHUNK))
    for lo, hi in zip(starts, starts[1:] + [rows]):
        up_gate = _dot(a_ref[lo:hi, :], w)
        up, gate = up_gate[:, :tf], up_gate[:, tf:]
        ext = jnp.concatenate([tail, up], axis=0)
        up_m1 = pltpu.roll(ext, 1, axis=0)[8:]
        up_m2 = pltpu.roll(ext, 2, axis=0)[8:]
        ch = cb + cw[0:1, :] * up_m2 + cw[1:2, :] * up_m1 + cw[2:3, :] * up
        hid_ref[lo:hi, :] = (ch * (1.0 + jnp.tanh(ch)) * gate).astype(hid_ref.dtype)
        tail = up[hi - lo - 8:]


def _ffn_in(a, w_ffn_in, conv_w, conv_b, w_ffn_out, *, rows, tf):
    m, k = a.shape
    nf = D_FF // tf
    n_steps = (m // rows) * nf
    d_out = w_ffn_out.shape[1]
    slab = D_FF // n_steps
    assert slab * n_steps == D_FF and slab % 16 == 0
    return pl.pallas_call(
        _ffn_in_kernel,
        out_shape=(jax.ShapeDtypeStruct((m, D_FF), BF16),
                   jax.ShapeDtypeStruct((D_FF, d_out), BF16)),
        grid=(m // rows, nf),
        in_specs=[
            pl.BlockSpec((rows, k), lambda i, j: (i, 0), pipeline_mode=pl.Buffered(1)),
            pl.BlockSpec((k, tf), lambda i, j: (0, j)),
            pl.BlockSpec((k, tf), lambda i, j: (0, nf + j)),
            pl.BlockSpec((CONV_W, tf), lambda i, j: (0, j)),
            pl.BlockSpec((1, tf), lambda i, j: (0, j)),
            pl.BlockSpec((slab, d_out), lambda i, j: (i * nf + j, 0)),
        ],
        out_specs=(pl.BlockSpec((rows, tf), lambda i, j: (i, j)),
                   pl.BlockSpec((slab, d_out), lambda i, j: (i * nf + j, 0))),
        compiler_params=_params(("parallel", "parallel")),
        name="ffn_in_conv",
    )(a, w_ffn_in, w_ffn_in, conv_w, conv_b.reshape(1, D_FF), w_ffn_out)


def kernel(x, positions, meta_tokens, attn_norm, w_in, w_gate_up, b_gate, ret_norm, gla_norm,
           w_out, ffn_norm, w_ffn_in, conv_w, conv_b, w_ffn_out, final_norm):
    batch, seq, d = x.shape
    rows = N_META + seq
    m = batch * rows

    meta_tokens = meta_tokens.astype(x.dtype)
    pos = jnp.concatenate([
        jnp.broadcast_to(jnp.arange(N_META, dtype=jnp.int32), (batch, N_META)),
        positions.astype(jnp.int32) + N_META], axis=1).reshape(m, 1)
    half = RET_DK // 2
    inv_freq = (ROPE_BASE ** (-jnp.arange(half, dtype=F32) / half)).reshape(1, half)

    w_in_t = jnp.swapaxes(w_in, 1, 2)[0]

    cos, sin = _rope_tables(pos, inv_freq, rows=rows)

    norm_rows = rows // 3
    hn = _embed_norm(x, meta_tokens, attn_norm[0], rows=norm_rows, out_dtype=BF16)
    proj, z = _in_proj(hn, w_in_t, tm=rows, tn=512)

    y_gla = _gla(proj, z, w_gate_up[0].astype(BF16), b_gate[0].reshape(1, -1), gla_norm[0],
                 batch=batch, rows=rows)
    merged = _retention_merge(proj, y_gla, cos, sin, ret_norm[0], batch=batch, rows=rows)

    h1 = _out_proj(merged, w_out[0], x, meta_tokens, tn=256)
    h1n = _rmsnorm(h1, ffn_norm[0], rows=norm_rows, out_dtype=BF16)
    hidden, w_ffn_out_b = _ffn_in(h1n, w_ffn_in[0], conv_w[0], conv_b[0], w_ffn_out[0],
                                  rows=rows, tf=256)
    h2 = _matmul_res(hidden, w_ffn_out_b, h1, tm=norm_rows, tn=256, name="ffn_out")
    return _final_norm(h2, final_norm, batch=batch, rows_per_batch=rows, seq=seq, rows=512)
```
